```python
import math
import jax, jax.numpy as jnp
from jax import lax
import numpy as np

D_MODEL = 1024
BATCH = 8
SEQ = 2048
DEPTH = 2
DEC_BATCH = 128
DEC_SEQ = 4
PAST_LEN = 16384
PAGE_SIZE = 128

N_MIXERS = 2
N_RET_LAYERS = (DEPTH + 1) // 2
N_REC_LAYERS = DEPTH // 2
RET_HEADS = 4
RET_DK = D_MODEL // RET_HEADS
RET_DV = 2 * D_MODEL // RET_HEADS
RET_QK = RET_HEADS * RET_DK
RET_VDIM = RET_HEADS * RET_DV
RET_CHUNK = 128
ROPE_BASE = 10000.0
D_RNN = 1280
LRU_BLOCKS = 8
LRU_BS = D_RNN // LRU_BLOCKS
CONV_W = 4
LRU_C = 8.0
D_FF = 2816
ALPHA = (2.0 * DEPTH) ** 0.25
BETA = (8.0 * DEPTH) ** -0.25
LN_EPS = 1e-5
GN_EPS = 1e-6

kernel_name = 'retnet_hawk_macaron_deepnorm_step'


def layer_norm(x, g, b):
    xf = x.astype(jnp.float32)
    mu = jnp.mean(xf, -1, keepdims=True)
    var = jnp.mean(jnp.square(xf - mu), -1, keepdims=True)
    return ((xf - mu) * lax.rsqrt(var + LN_EPS) * g + b).astype(x.dtype)


def swiglu(x, w_in, w_out):
    g, u = jnp.split(x @ w_in, 2, axis=-1)
    return (jax.nn.silu(g) * u) @ w_out


def rotary(x, pos):
    half = x.shape[-1] // 2
    inv = ROPE_BASE ** (-jnp.arange(half, dtype=jnp.float32) / half)
    ang = pos.astype(jnp.float32)[:, None] * inv[None, :]
    cos, sin = jnp.cos(ang), jnp.sin(ang)
    x1, x2 = x[..., :half], x[..., half:]
    return jnp.concatenate([x1 * cos - x2 * sin, x1 * sin + x2 * cos], axis=-1)


def retention_log_gamma():
    return jnp.log1p(-jnp.exp2(-5.0 - jnp.arange(RET_HEADS, dtype=jnp.float32)))


def retention_chunkwise(q, k, v, s0):
    B, H, T, dk = q.shape
    dv = v.shape[-1]
    C = math.gcd(T, RET_CHUNK)
    n = T // C
    lg = retention_log_gamma()
    idx = jnp.arange(C, dtype=jnp.float32)
    rel = idx[:, None] - idx[None, :]
    dmask = jnp.where(rel >= 0, jnp.exp(lg[:, None, None] * jnp.maximum(rel, 0.0)), 0.0)
    q_dec = jnp.exp(lg[:, None] * (idx + 1.0))[:, :, None]
    k_dec = jnp.exp(lg[:, None] * (C - 1.0 - idx))[:, :, None]
    chunk_dec = jnp.exp(lg * C)[:, None, None]

    def to_chunks(a):
        return a.reshape(B, H, n, C, a.shape[-1]).transpose(2, 0, 1, 3, 4)

    def step(s, qkv):
        qc, kc, vc = qkv
        scores = jnp.einsum('bhcd,bhed->bhce', qc, kc) * dmask
        o = (jnp.einsum('bhce,bhev->bhcv', scores, vc)
             + jnp.einsum('bhcd,bhdv->bhcv', qc * q_dec, s))
        s = s * chunk_dec + jnp.einsum('bhcd,bhcv->bhdv', kc * k_dec, vc)
        return s, o

    s, o = lax.scan(step, s0, (to_chunks(q), to_chunks(k), to_chunks(v)))
    o = o.transpose(1, 2, 0, 3, 4).reshape(B, H, T, dv)
    return o, s


def retention_mixer(x, pos, s0, w_in, gn_g, w_out):
    B, T, _ = x.shape
    proj = x @ w_in
    q, k, v, g = jnp.split(proj, [RET_QK, 2 * RET_QK, 2 * RET_QK + RET_VDIM], axis=-1)

    def heads(a, d):
        return a.reshape(B, T, RET_HEADS, d).transpose(0, 2, 1, 3).astype(jnp.float32)

    q = rotary(heads(q, RET_DK), pos)
    k = rotary(heads(k, RET_DK), pos) * (RET_DK ** -0.5)
    v = heads(v, RET_DV)
    o, s = retention_chunkwise(q, k, v, s0.astype(jnp.float32))
    mu = jnp.mean(o, -1, keepdims=True)
    var = jnp.mean(jnp.square(o - mu), -1, keepdims=True)
    o = ((o - mu) * lax.rsqrt(var + GN_EPS)).transpose(0, 2, 1, 3).reshape(B, T, RET_VDIM) * gn_g
    y = (jax.nn.silu(g.astype(jnp.float32)) * o).astype(x.dtype) @ w_out
    return y, s


def rglru_mixer(x, conv_buf, h0, w_in, conv_w, conv_b, w_a, b_a, w_i, b_i, lam, w_out):
    B, T, _ = x.shape
    gate_br, xb = jnp.split(x @ w_in, 2, axis=-1)
    gate_br = jax.nn.gelu(gate_br, approximate=True)
    xp = jnp.concatenate([conv_buf.astype(xb.dtype), xb], axis=1)
    xc = conv_b + sum(conv_w[j] * xp[:, j:j + T] for j in range(CONV_W))
    new_buf = xp[:, T:]
    xcb = xc.reshape(B, T, LRU_BLOCKS, LRU_BS)
    r = jax.nn.sigmoid(jnp.einsum('btni,nij->btnj', xcb, w_a).reshape(B, T, D_RNN) + b_a)
    i = jax.nn.sigmoid(jnp.einsum('btni,nij->btnj', xcb, w_i).reshape(B, T, D_RNN) + b_i)
    log_a = -LRU_C * r.astype(jnp.float32) * jax.nn.softplus(-lam.astype(jnp.float32))
    a = jnp.exp(log_a)
    u = jnp.sqrt(-jnp.expm1(2.0 * log_a)) * (i * xc).astype(jnp.float32)

    def step(h, au):
        a_t, u_t = au
        h = a_t * h + u_t
        return h, h

    h_last, hs = lax.scan(step, h0.astype(jnp.float32), (a.transpose(1, 0, 2), u.transpose(1, 0, 2)))
    hs = hs.transpose(1, 0, 2).astype(x.dtype)
    y = (gate_br * hs) @ w_out
    return y, new_buf, h_last


def trunk(x, pos, ret_states, conv_states, lru_states, ln_g, ln_b, ffn1_w_in, ffn1_w_out,
          ffn2_w_in, ffn2_w_out, ret_w_in, ret_gn_g, ret_w_out, rec_w_in, rec_conv_w, rec_conv_b,
          rec_w_a, rec_b_a, rec_w_i, rec_b_i, rec_lam, rec_w_out):
    new_ret, new_conv, new_lru = [], [], []
    for layer in range(DEPTH):
        j = layer // N_MIXERS
        x = layer_norm(ALPHA * x + 0.5 * swiglu(x, ffn1_w_in[layer], ffn1_w_out[layer]),
                       ln_g[layer, 0], ln_b[layer, 0])
        if layer % N_MIXERS == 0:
            m, s = retention_mixer(x, pos, ret_states[j], ret_w_in[j], ret_gn_g[j], ret_w_out[j])
            new_ret.append(s)
        else:
            m, cb, h = rglru_mixer(x, conv_states[j], lru_states[j], rec_w_in[j], rec_conv_w[j],
                                   rec_conv_b[j], rec_w_a[j], rec_b_a[j], rec_w_i[j], rec_b_i[j],
                                   rec_lam[j], rec_w_out[j])
            new_conv.append(cb)
            new_lru.append(h)
        x = layer_norm(ALPHA * x + m, ln_g[layer, 1], ln_b[layer, 1])
        x = layer_norm(ALPHA * x + 0.5 * swiglu(x, ffn2_w_in[layer], ffn2_w_out[layer]),
                       ln_g[layer, 2], ln_b[layer, 2])
    return x, jnp.stack(new_ret), jnp.stack(new_conv), jnp.stack(new_lru)


def setup_inputs(seed: int = 0) -> dict:
    key = jax.random.key(seed)
    ks = jax.random.split(key, 32)
    f32 = jnp.float32
    nrm = lambda k, shape, s: jax.random.normal(k, shape, f32) * s
    u = jax.random.uniform(ks[24], (N_REC_LAYERS, D_RNN), f32, 0.9, 0.999)
    a0 = u ** (1.0 / LRU_C)
    return {
        'x_prompt': nrm(ks[0], (BATCH, SEQ, D_MODEL), 1.0),
        'x_sample': nrm(ks[1], (DEC_BATCH, DEC_SEQ, D_MODEL), 1.0),
        'state_ret': nrm(ks[2], (N_RET_LAYERS, DEC_BATCH, RET_HEADS, RET_DK, RET_DV), 0.5),
        'state_conv': nrm(ks[3], (N_REC_LAYERS, DEC_BATCH, CONV_W - 1, D_RNN), 1.0),
        'state_lru': nrm(ks[4], (N_REC_LAYERS, DEC_BATCH, D_RNN), 0.5),
        'ln_g': 1.0 + nrm(ks[5], (DEPTH, 3, D_MODEL), 0.02),
        'ln_b': nrm(ks[6], (DEPTH, 3, D_MODEL), 0.02),
        'ffn1_w_in': nrm(ks[7], (DEPTH, D_MODEL, 2 * D_FF), D_MODEL ** -0.5),
        'ffn1_w_out': nrm(ks[8], (DEPTH, D_FF, D_MODEL), BETA * D_FF ** -0.5),
        'ffn2_w_in': nrm(ks[9], (DEPTH, D_MODEL, 2 * D_FF), D_MODEL ** -0.5),
        'ffn2_w_out': nrm(ks[10], (DEPTH, D_FF, D_MODEL), BETA * D_FF ** -0.5),
        'ret_w_in': nrm(ks[11], (N_RET_LAYERS, D_MODEL, 2 * RET_QK + 2 * RET_VDIM), D_MODEL ** -0.5),
        'ret_gn_g': 1.0 + nrm(ks[12], (N_RET_LAYERS, RET_VDIM), 0.02),
        'ret_w_out': nrm(ks[13], (N_RET_LAYERS, RET_VDIM, D_MODEL), BETA * RET_VDIM ** -0.5),
        'rec_w_in': nrm(ks[14], (N_REC_LAYERS, D_MODEL, 2 * D_RNN), D_MODEL ** -0.5),
        'rec_conv_w': nrm(ks[15], (N_REC_LAYERS, CONV_W, D_RNN), CONV_W ** -0.5),
        'rec_conv_b': nrm(ks[16], (N_REC_LAYERS, D_RNN), 0.02),
        'rec_w_a': nrm(ks[17], (N_REC_LAYERS, LRU_BLOCKS, LRU_BS, LRU_BS), LRU_BS ** -0.5),
        'rec_b_a': nrm(ks[18], (N_REC_LAYERS, D_RNN), 0.02),
        'rec_w_i': nrm(ks[19], (N_REC_LAYERS, LRU_BLOCKS, LRU_BS, LRU_BS), LRU_BS ** -0.5),
        'rec_b_i': nrm(ks[20], (N_REC_LAYERS, D_RNN), 0.02),
        'rec_lam': jnp.log(a0) - jnp.log1p(-a0),
        'rec_w_out': nrm(ks[21], (N_REC_LAYERS, D_RNN, D_MODEL), BETA * D_RNN ** -0.5),
    }


def reference(x_prompt, x_sample, state_ret, state_conv, state_lru, ln_g, ln_b, ffn1_w_in,
              ffn1_w_out, ffn2_w_in, ffn2_w_out, ret_w_in, ret_gn_g, ret_w_out, rec_w_in,
              rec_conv_w, rec_conv_b, rec_w_a, rec_b_a, rec_w_i, rec_b_i, rec_lam, rec_w_out):
    weights = (ln_g, ln_b, ffn1_w_in, ffn1_w_out, ffn2_w_in, ffn2_w_out, ret_w_in, ret_gn_g,
               ret_w_out, rec_w_in, rec_conv_w, rec_conv_b, rec_w_a, rec_b_a, rec_w_i, rec_b_i,
               rec_lam, rec_w_out)
    bp, tp, _ = x_prompt.shape
    ts = x_sample.shape[1]
    zero_ret = jnp.zeros((N_RET_LAYERS, bp, RET_HEADS, RET_DK, RET_DV), jnp.float32)
    zero_conv = jnp.zeros((N_REC_LAYERS, bp, CONV_W - 1, D_RNN), x_prompt.dtype)
    zero_lru = jnp.zeros((N_REC_LAYERS, bp, D_RNN), jnp.float32)
    pos_prompt = jnp.arange(tp, dtype=jnp.int32)
    y_prompt, ret_p, conv_p, lru_p = trunk(x_prompt, pos_prompt, zero_ret, zero_conv, zero_lru, *weights)
    pos_sample = PAST_LEN + jnp.arange(ts, dtype=jnp.int32)
    y_sample, ret_s, conv_s, lru_s = trunk(x_sample, pos_sample, state_ret, state_conv, state_lru, *weights)
    return (y_prompt, y_sample, ret_p, conv_p, lru_p, ret_s, conv_s, lru_s)
```

```python
import functools
import math

import numpy as np
import jax
import jax.numpy as jnp
from jax import lax
from jax.experimental import pallas as pl
from jax.experimental.pallas import tpu as pltpu

F32 = jnp.float32
BF16 = jnp.bfloat16

DEPTH = 2
PAST_LEN = 16384
RET_HEADS = 4
RET_CHUNK = 128
ROPE_BASE = 10000.0
LRU_BLOCKS = 8
CONV_W = 4
LRU_C = 8.0
ALPHA = (2.0 * DEPTH) ** 0.25
LN_EPS = 1e-5
GN_EPS = 1e-6

V7X_SCOPED_VMEM_BYTES = 60000 * 1024
LANES = 128
ROW_TILE = 512
FF_CHUNK = 256


def _params(semantics, vmem_bytes):
    return pltpu.CompilerParams(
        dimension_semantics=semantics,
        vmem_limit_bytes=int(min(V7X_SCOPED_VMEM_BYTES, vmem_bytes)),
    )


def _resident(shape):
    zeros = (0,) * len(shape)
    return pl.BlockSpec(shape, lambda *_: zeros, pipeline_mode=pl.Buffered(1))


def _nbytes(shape, dtype):
    return int(np.prod(shape)) * jnp.dtype(dtype).itemsize


def _layer_norm(y, g, b):
    mu = jnp.mean(y, axis=-1, keepdims=True)
    d = y - mu
    var = jnp.mean(d * d, axis=-1, keepdims=True)
    return d * lax.rsqrt(var + LN_EPS) * g + b


def _mm(a, b):
    return jnp.dot(a, b, preferred_element_type=F32)


def _ffn_ln_kernel(x_ref, wg_ref, wu_ref, wo_ref, g_ref, b_ref, o_ref, *, n_chunks):
    x = x_ref[...]
    xb = x.astype(BF16)
    acc = jnp.zeros(x.shape, F32)
    for c in range(n_chunks):
        gate = _mm(xb, wg_ref[c])
        up = _mm(xb, wu_ref[c])
        h = (gate * jax.nn.sigmoid(gate) * up).astype(BF16)
        acc = acc + _mm(h, wo_ref[c])
    o_ref[...] = _layer_norm(ALPHA * x + 0.5 * acc, g_ref[...], b_ref[...])


def _ffn_ln(x, w, ln_g, ln_b):
    wg, wu, wo = w
    m, d = x.shape
    n_chunks, _, ck = wg.shape
    tm = min(ROW_TILE, m)
    row = pl.BlockSpec((tm, d), lambda i: (i, 0))
    vmem = (4 * _nbytes((tm, d), F32) + 3 * _nbytes(wg.shape, BF16)
            + 8 * _nbytes((tm, d), F32))
    return pl.pallas_call(
        functools.partial(_ffn_ln_kernel, n_chunks=n_chunks),
        grid=(m // tm,),
        in_specs=[row, _resident(wg.shape), _resident(wu.shape), _resident(wo.shape),
                  _resident((1, d)), _resident((1, d))],
        out_specs=row,
        out_shape=jax.ShapeDtypeStruct((m, d), F32),
        compiler_params=_params(("parallel",), vmem),
        name="ffn_ln",
    )(x, wg, wu, wo, ln_g, ln_b)


def _proj_ln_kernel(a_ref, x_ref, w_ref, g_ref, b_ref, o_ref):
    y = ALPHA * x_ref[...] + _mm(a_ref[...], w_ref[...])
    o_ref[...] = _layer_norm(y, g_ref[...], b_ref[...])


def _proj_ln(a, x, w, ln_g, ln_b):
    m, d = x.shape
    k = a.shape[1]
    tm = min(ROW_TILE, m)
    vmem = (2 * _nbytes((tm, k), BF16) + 4 * _nbytes((tm, d), F32) + _nbytes(w.shape, BF16)
            + 4 * _nbytes((tm, d), F32))
    return pl.pallas_call(
        _proj_ln_kernel,
        grid=(m // tm,),
        in_specs=[pl.BlockSpec((tm, k), lambda i: (i, 0)), pl.BlockSpec((tm, d), lambda i: (i, 0)),
                  _resident(w.shape), _resident((1, d)), _resident((1, d))],
        out_specs=pl.BlockSpec((tm, d), lambda i: (i, 0)),
        out_shape=jax.ShapeDtypeStruct((m, d), F32),
        compiler_params=_params(("parallel",), vmem),
        name="proj_ln",
    )(a, x, w, ln_g, ln_b)


def _ret_proj_kernel(x_ref, wq_ref, wk_ref, wv_ref, wg_ref, cos_ref, sin_ref,
                     q_ref, k_ref, v_ref, g_ref, *, heads, k_scale):
    xb = x_ref[...].astype(BF16)
    cos = cos_ref[...]
    sin = sin_ref[...]
    half = cos.shape[1]
    for w_ref, dst, scale in ((wq_ref, q_ref, None), (wk_ref, k_ref, k_scale)):
        p = _mm(xb, w_ref[...])
        for h in range(heads):
            lo = 2 * half * h
            x1 = p[:, lo:lo + half]
            x2 = p[:, lo + half:lo + 2 * half]
            r1 = x1 * cos - x2 * sin
            r2 = x1 * sin + x2 * cos
            if scale is not None:
                r1 = r1 * scale
                r2 = r2 * scale
            dst[:, lo:lo + half] = r1
            dst[:, lo + half:lo + 2 * half] = r2
    v_ref[...] = _mm(xb, wv_ref[...]).astype(BF16)
    g_ref[...] = _mm(xb, wg_ref[...])


def _ret_proj(x, w, cos, sin, dk):
    wq, wk, wv, wg = w
    m, d = x.shape
    qk, vd = wq.shape[1], wv.shape[1]
    tm = min(ROW_TILE, m)
    if cos.shape[0] < tm:
        reps = tm // cos.shape[0]
        cos, sin = jnp.tile(cos, (reps, 1)), jnp.tile(sin, (reps, 1))
    n_tab = cos.shape[0] // tm
    row = lambda n: pl.BlockSpec((tm, n), lambda i: (i, 0))
    tab = pl.BlockSpec((tm, cos.shape[1]), lambda i: (i % n_tab, 0))
    vmem = (2 * _nbytes((tm, d), F32) + _nbytes((d, 2 * qk + 2 * vd), BF16)
            + 2 * (2 * _nbytes((tm, qk), F32) + _nbytes((tm, vd), BF16) + _nbytes((tm, vd), F32))
            + 4 * _nbytes((tm, vd), F32))
    return pl.pallas_call(
        functools.partial(_ret_proj_kernel, heads=RET_HEADS, k_scale=dk ** -0.5),
        grid=(m // tm,),
        in_specs=[row(d), _resident(wq.shape), _resident(wk.shape), _resident(wv.shape),
                  _resident(wg.shape), tab, tab],
        out_specs=[row(qk), row(qk), row(vd), row(vd)],
        out_shape=[jax.ShapeDtypeStruct((m, qk), F32), jax.ShapeDtypeStruct((m, qk), F32),
                   jax.ShapeDtypeStruct((m, vd), BF16), jax.ShapeDtypeStruct((m, vd), F32)],
        compiler_params=_params(("parallel",), vmem),
        name="ret_proj",
    )(x, wq, wk, wv, wg, cos, sin)


def _ret_core_kernel(*refs, heads, chunk, zero_init):
    if zero_init:
        q_ref, k_ref, v_ref, g_ref, gn_ref, dm_ref, qd_ref, kd_ref, cd_ref, o_ref, s_ref = refs
    else:
        q_ref, k_ref, v_ref, g_ref, gn_ref, dm_ref, qd_ref, kd_ref, cd_ref, s0_ref, o_ref, s_ref = refs
    bb, tt, _ = q_ref.shape
    dk = q_ref.shape[2] // heads
    dv = v_ref.shape[2] // heads

    @pl.when(pl.program_id(1) == 0)
    def _():
        if zero_init:
            s_ref[...] = jnp.zeros(s_ref.shape, F32)
        else:
            s_ref[...] = s0_ref[...]

    for h in range(heads):
        dmask = dm_ref[h]
        q_dec = qd_ref[h]
        k_dec = kd_ref[h]
        chunk_dec = cd_ref[h]
        gn = gn_ref[:, h * dv:(h + 1) * dv]
        for b in range(bb):
            for c in range(tt // chunk):
                rows = slice(c * chunk, (c + 1) * chunk)
                qc = q_ref[b, rows, h * dk:(h + 1) * dk]
                kc = k_ref[b, rows, h * dk:(h + 1) * dk]
                vc = v_ref[b, rows, h * dv:(h + 1) * dv]
                s = s_ref[b, h]
                scores = lax.dot_general(qc.astype(BF16), kc.astype(BF16),
                                         (((1,), (1,)), ((), ())),
                                         preferred_element_type=F32) * dmask
                o = _mm(scores.astype(BF16), vc) + _mm((qc * q_dec).astype(BF16), s.astype(BF16))
                s_ref[b, h] = s * chunk_dec + lax.dot_general(
                    (kc * k_dec).astype(BF16), vc, (((0,), (0,)), ((), ())),
                    preferred_element_type=F32)
                mu = jnp.mean(o, axis=-1, keepdims=True)
                d = o - mu
                var = jnp.mean(d * d, axis=-1, keepdims=True)
                on = d * lax.rsqrt(var + GN_EPS) * gn
                gate = g_ref[b, rows, h * dv:(h + 1) * dv]
                o_ref[b, rows, h * dv:(h + 1) * dv] = (
                    gate * jax.nn.sigmoid(gate) * on).astype(BF16)


def _decay_tables(chunk):
    lg = jnp.log1p(-jnp.exp2(-5.0 - jnp.arange(RET_HEADS, dtype=F32)))
    idx = jnp.arange(chunk, dtype=F32)
    rel = idx[:, None] - idx[None, :]
    dmask = jnp.where(rel >= 0, jnp.exp(lg[:, None, None] * jnp.maximum(rel, 0.0)), 0.0)
    q_dec = jnp.exp(lg[:, None] * (idx + 1.0))[:, :, None]
    k_dec = jnp.exp(lg[:, None] * (chunk - 1.0 - idx))[:, :, None]
    chunk_dec = jnp.exp(lg * chunk)[:, None, None]
    return dmask, q_dec, k_dec, chunk_dec


def _ret_core(q, k, v, g, gn_g, s0, *, bb, tt, chunk):
    b, t, qk = q.shape
    vd = v.shape[2]
    heads = RET_HEADS
    dk, dv = qk // heads, vd // heads
    dmask, q_dec, k_dec, chunk_dec = _decay_tables(chunk)
    blk = lambda n: pl.BlockSpec((bb, tt, n), lambda i, j: (i, j, 0))
    st = pl.BlockSpec((bb, heads, dk, dv), lambda i, j: (i, 0, 0, 0))
    in_specs = [blk(qk), blk(qk), blk(vd), blk(vd), _resident((1, vd)),
                _resident(dmask.shape), _resident(q_dec.shape), _resident(k_dec.shape),
                _resident(chunk_dec.shape)]
    args = [q, k, v, g, gn_g, dmask, q_dec, k_dec, chunk_dec]
    if s0 is not None:
        in_specs.append(st)
        args.append(s0)
    n_state = 2 if s0 is None else 4
    vmem = (2 * (2 * _nbytes((bb, tt, qk), F32) + 2 * _nbytes((bb, tt, vd), BF16)
                 + _nbytes((bb, tt, vd), F32))
            + n_state * _nbytes((bb, heads, dk, dv), F32)
            + 16 * _nbytes((max(chunk, 8), dv), F32) + 8 * _nbytes((dk, dv), F32))
    return pl.pallas_call(
        functools.partial(_ret_core_kernel, heads=heads, chunk=chunk, zero_init=s0 is None),
        grid=(b // bb, t // tt),
        in_specs=in_specs,
        out_specs=[blk(vd), st],
        out_shape=[jax.ShapeDtypeStruct((b, t, vd), BF16),
                   jax.ShapeDtypeStruct((b, heads, dk, dv), F32)],
        compiler_params=_params(("parallel", "arbitrary"), vmem),
        name="ret_core",
    )(*args)


def _gelu_tanh(x):
    return 0.5 * x * (1.0 + jnp.tanh(math.sqrt(2.0 / math.pi) * (x + 0.044715 * (x * x * x))))


def _rglru_kernel(x_ref, wgate_ref, wx_ref, cw_ref, cb_ref, wbd_ref, ba_ref, bi_ref, lam_ref,
                  conv0_ref, h0_ref, y_ref, convn_ref, hlast_ref,
                  xpad_ref, a_ref, u_ref, h_ref, *, nb):
    tm = x_ref.shape[0]
    hist = (CONV_W - 1) * nb
    dr = wx_ref.shape[1]
    half = dr // 2

    @pl.when(pl.program_id(0) == 0)
    def _():
        xpad_ref[0:hist, :] = conv0_ref[...]
        h_ref[...] = h0_ref[...]

    xb = x_ref[...].astype(BF16)
    gate = _gelu_tanh(_mm(xb, wgate_ref[...]))
    xpad_ref[hist:hist + tm, :] = _mm(xb, wx_ref[...])

    conv = None
    for j in range(CONV_W):
        term = cw_ref[j:j + 1, :] * xpad_ref[j * nb:j * nb + tm, :]
        conv = term if conv is None else conv + term
    xc = cb_ref[...] + conv

    tail = xpad_ref[tm:tm + hist, :]
    convn_ref[...] = tail
    xpad_ref[0:hist, :] = tail

    lam = lam_ref[...]
    softplus_neg_lam = jnp.maximum(-lam, 0.0) + jnp.log1p(jnp.exp(-jnp.abs(lam)))
    for hf in range(2):
        cols = slice(hf * half, (hf + 1) * half)
        xh = xc[:, cols]
        z = _mm(xh.astype(BF16), wbd_ref[hf])
        r = jax.nn.sigmoid(z[:, :half] + ba_ref[:, cols])
        ig = jax.nn.sigmoid(z[:, half:] + bi_ref[:, cols])
        log_a = -LRU_C * r * softplus_neg_lam[:, cols]
        a_ref[:, cols] = jnp.exp(log_a)
        u_ref[:, cols] = jnp.sqrt(1.0 - jnp.exp(2.0 * log_a)) * (ig * xh)

    h = h_ref[...]
    for t in range(tm // nb):
        rows = slice(t * nb, (t + 1) * nb)
        h = a_ref[rows, :] * h + u_ref[rows, :]
        u_ref[rows, :] = h
    h_ref[...] = h
    hlast_ref[...] = h
    y_ref[...] = (gate * u_ref[...]).astype(BF16)


def _rglru(x_tm, w, conv0_tm, h0, *, nb):
    wgate, wx, cw, cb, wbd, ba, bi, lam = w
    m, d = x_tm.shape
    dr = wx.shape[1]
    tm = min(ROW_TILE, m)
    hist = (CONV_W - 1) * nb
    assert tm % nb == 0 and tm >= hist
    vmem = (2 * _nbytes((tm, d), F32) + 2 * _nbytes((d, dr), BF16) + _nbytes(wbd.shape, BF16)
            + 2 * _nbytes((tm, dr), BF16) + 4 * _nbytes((hist + nb, dr), F32)
            + _nbytes((tm + hist, dr), F32) + 2 * _nbytes((tm, dr), F32) + _nbytes((nb, dr), F32)
            + 8 * _nbytes((tm, dr), F32))
    return pl.pallas_call(
        functools.partial(_rglru_kernel, nb=nb),
        grid=(m // tm,),
        in_specs=[pl.BlockSpec((tm, d), lambda i: (i, 0)),
                  _resident(wgate.shape), _resident(wx.shape), _resident(cw.shape),
                  _resident(cb.shape), _resident(wbd.shape), _resident(ba.shape),
                  _resident(bi.shape), _resident(lam.shape),
                  _resident(conv0_tm.shape), _resident(h0.shape)],
        out_specs=[pl.BlockSpec((tm, dr), lambda i: (i, 0)),
                   pl.BlockSpec((hist, dr), lambda i: (0, 0)),
                   pl.BlockSpec((nb, dr), lambda i: (0, 0))],
        out_shape=[jax.ShapeDtypeStruct((m, dr), BF16),
                   jax.ShapeDtypeStruct((hist, dr), F32),
                   jax.ShapeDtypeStruct((nb, dr), F32)],
        scratch_shapes=[pltpu.VMEM((tm + hist, dr), F32), pltpu.VMEM((tm, dr), F32),
                        pltpu.VMEM((tm, dr), F32), pltpu.VMEM((nb, dr), F32)],
        compiler_params=_params(("arbitrary",), vmem),
        name="rglru",
    )(x_tm, wgate, wx, cw, cb, wbd, ba, bi, lam, conv0_tm, h0)


def _prep_ffn(w_in, w_out):
    d, two_f = w_in.shape
    f = two_f // 2
    n = f // FF_CHUNK
    chunks = lambda w: w.astype(BF16).reshape(d, n, FF_CHUNK).transpose(1, 0, 2)
    return chunks(w_in[:, :f]), chunks(w_in[:, f:]), w_out.astype(BF16).reshape(n, FF_CHUNK, d)


def _prep_ret(w_in, w_out):
    qk = w_out.shape[1]
    vd = w_out.shape[0]
    w = w_in.astype(BF16)
    return (w[:, :qk], w[:, qk:2 * qk], w[:, 2 * qk:2 * qk + vd], w[:, 2 * qk + vd:]), w_out.astype(BF16)


def _block_diag(blocks):
    n, bs, _ = blocks.shape
    eye = jnp.eye(n, dtype=blocks.dtype)
    return (eye[:, None, :, None] * blocks[:, :, None, :]).reshape(n * bs, n * bs)


def _prep_rec(w_in, conv_w, conv_b, w_a, b_a, w_i, b_i, lam, w_out):
    dr = w_out.shape[0]
    w = w_in.astype(BF16)
    nblk = w_a.shape[0] // 2
    halves = [jnp.concatenate([_block_diag(w_a[s:s + nblk]), _block_diag(w_i[s:s + nblk])], axis=1)
              for s in (0, nblk)]
    wbd = jnp.stack(halves).astype(BF16)
    row = lambda v: v.reshape(1, dr)
    return (w[:, :dr], w[:, dr:], conv_w, row(conv_b), wbd, row(b_a), row(b_i), row(lam)), w_out.astype(BF16)


def _rope_tables(pos, dk):
    half = dk // 2
    inv = ROPE_BASE ** (-jnp.arange(half, dtype=F32) / half)
    ang = pos.astype(F32)[:, None] * inv[None, :]
    return jnp.cos(ang), jnp.sin(ang)


def _to_time_major(a, b, t):
    return a.reshape(b, t, a.shape[-1]).transpose(1, 0, 2).reshape(t * b, a.shape[-1])


def _to_batch_major(a, b, t):
    return a.reshape(t, b, a.shape[-1]).transpose(1, 0, 2).reshape(b * t, a.shape[-1])


def _trunk(x, pos, state_ret, state_conv, state_lru, ln_g, ln_b, ffn1, ffn2, ret, rec):
    b, t, d = x.shape
    m = b * t
    x = x.reshape(m, d)
    ln = lambda layer, k: (ln_g[layer, k].reshape(1, d), ln_b[layer, k].reshape(1, d))
    new_ret, new_conv, new_lru = [], [], []
    for layer in range(DEPTH):
        j = layer // 2
        x = _ffn_ln(x, ffn1[layer], *ln(layer, 0))
        if layer % 2 == 0:
            (w_proj, w_out), gn_g = ret[j]
            dk = w_proj[0].shape[1] // RET_HEADS
            cos, sin = _rope_tables(pos, dk)
            chunk = math.gcd(t, RET_CHUNK)
            if t >= ROW_TILE:
                bb, tt = 1, ROW_TILE
            else:
                bb, tt = 4, t
            q, k, v, g = _ret_proj(x, w_proj, cos, sin, dk)
            s0 = None if state_ret is None else state_ret[j]
            gated, s_new = _ret_core(q.reshape(b, t, -1), k.reshape(b, t, -1), v.reshape(b, t, -1),
                                     g.reshape(b, t, -1), gn_g, s0, bb=bb, tt=tt, chunk=chunk)
            new_ret.append(s_new)
            mix_in = gated.reshape(m, -1)
        else:
            w_rec, w_out = rec[j]
            dr = w_out.shape[0]
            if state_conv is None:
                conv0 = jnp.zeros(((CONV_W - 1) * b, dr), F32)
                h0 = jnp.zeros((b, dr), F32)
            else:
                conv0 = state_conv[j].transpose(1, 0, 2).reshape((CONV_W - 1) * b, dr)
                h0 = state_lru[j]
            y_tm, conv_tm, h_last = _rglru(_to_time_major(x, b, t), w_rec, conv0, h0, nb=b)
            new_conv.append(conv_tm.reshape(CONV_W - 1, b, dr).transpose(1, 0, 2))
            new_lru.append(h_last)
            mix_in = _to_batch_major(y_tm, b, t)
        x = _proj_ln(mix_in, x, w_out, *ln(layer, 1))
        x = _ffn_ln(x, ffn2[layer], *ln(layer, 2))
    return x.reshape(b, t, d), jnp.stack(new_ret), jnp.stack(new_conv), jnp.stack(new_lru)


def kernel(x_prompt, x_sample, state_ret, state_conv, state_lru, ln_g, ln_b, ffn1_w_in, ffn1_w_out,
           ffn2_w_in, ffn2_w_out, ret_w_in, ret_gn_g, ret_w_out, rec_w_in, rec_conv_w, rec_conv_b,
           rec_w_a, rec_b_a, rec_w_i, rec_b_i, rec_lam, rec_w_out):
    ffn1 = [_prep_ffn(ffn1_w_in[l], ffn1_w_out[l]) for l in range(DEPTH)]
    ffn2 = [_prep_ffn(ffn2_w_in[l], ffn2_w_out[l]) for l in range(DEPTH)]
    ret = [(_prep_ret(ret_w_in[j], ret_w_out[j]), ret_gn_g[j].reshape(1, -1))
           for j in range(ret_w_in.shape[0])]
    rec = [_prep_rec(rec_w_in[j], rec_conv_w[j], rec_conv_b[j], rec_w_a[j], rec_b_a[j], rec_w_i[j],
                     rec_b_i[j], rec_lam[j], rec_w_out[j]) for j in range(rec_w_in.shape[0])]
    weights = (ln_g, ln_b, ffn1, ffn2, ret, rec)
    tp, ts = x_prompt.shape[1], x_sample.shape[1]
    y_p, ret_p, conv_p, lru_p = _trunk(x_prompt, jnp.arange(tp, dtype=jnp.int32),
                                       None, None, None, *weights)
    y_s, ret_s, conv_s, lru_s = _trunk(x_sample, PAST_LEN + jnp.arange(ts, dtype=jnp.int32),
                                       state_ret, state_conv, state_lru, *weights)
    return (y_p, y_s, ret_p, conv_p, lru_p, ret_s, conv_s, lru_s)
```

```python
import functools
import math

import numpy as np
import jax
import jax.numpy as jnp
from jax import lax
from jax.experimental import pallas as pl
from jax.experimental.pallas import tpu as pltpu

F32 = jnp.float32
BF16 = jnp.bfloat16

DEPTH = 2
PAST_LEN = 16384
RET_HEADS = 4
RET_CHUNK = 128
ROPE_BASE = 10000.0
CONV_W = 4
LRU_C = 8.0
ALPHA = (2.0 * DEPTH) ** 0.25
LN_EPS = 1e-5
GN_EPS = 1e-6

V7X_SCOPED_VMEM_BYTES = 60000 * 1024
ROW_TILE = 512
FF_CHUNK = 256
SHORT_SEQ_BATCH_BLOCK = 4


def _params(semantics, vmem_bytes):
    return pltpu.CompilerParams(
        dimension_semantics=semantics,
        vmem_limit_bytes=int(min(V7X_SCOPED_VMEM_BYTES, vmem_bytes)),
    )


def _resident(shape):
    zeros = (0,) * len(shape)
    return pl.BlockSpec(shape, lambda *_: zeros, pipeline_mode=pl.Buffered(1))


def _nbytes(shape, dtype):
    return int(np.prod(shape)) * jnp.dtype(dtype).itemsize


def _layer_norm(y, g, b):
    mu = jnp.mean(y, axis=-1, keepdims=True)
    d = y - mu
    var = jnp.mean(d * d, axis=-1, keepdims=True)
    return d * lax.rsqrt(var + LN_EPS) * g + b


def _mm(a, b):
    return jnp.dot(a, b, preferred_element_type=F32)


def _sigmoid(x):
    return 0.5 * jnp.tanh(0.5 * x) + 0.5


def _ffn_ln_kernel(x_ref, win_ref, wout_ref, g_ref, b_ref, o_ref, *, ff):
    x = x_ref[...]
    xb = x.astype(BF16)
    acc = jnp.zeros(x.shape, F32)
    for lo in range(0, ff, FF_CHUNK):
        gate = _mm(xb, win_ref[:, lo:lo + FF_CHUNK])
        up = _mm(xb, win_ref[:, ff + lo:ff + lo + FF_CHUNK])
        h = (gate * jax.nn.sigmoid(gate) * up).astype(BF16)
        acc = acc + _mm(h, wout_ref[lo:lo + FF_CHUNK, :])
    o_ref[...] = _layer_norm(ALPHA * x + 0.5 * acc, g_ref[...], b_ref[...])


def _ffn_ln(x, w, ln_g, ln_b, *, in_tm=False, out_tm=False, nb=1):
    w_in, w_out = w
    m, d = x.shape
    ff = w_out.shape[0]
    assert ff % FF_CHUNK == 0
    tm = min(ROW_TILE, m)
    n_t = m // nb // tm
    bm_spec = pl.BlockSpec((tm, d), lambda i: (i, 0))
    tm_spec = pl.BlockSpec((tm, d), lambda i: (i % n_t, i // n_t))
    tm_shape = (m // nb, nb * d)
    vmem = (4 * _nbytes((tm, d), F32) + _nbytes(w_in.shape, BF16) + _nbytes(w_out.shape, BF16)
            + 8 * _nbytes((tm, d), F32))
    out = pl.pallas_call(
        functools.partial(_ffn_ln_kernel, ff=ff),
        grid=(m // tm,),
        in_specs=[tm_spec if in_tm else bm_spec, _resident(w_in.shape), _resident(w_out.shape),
                  _resident((1, d)), _resident((1, d))],
        out_specs=tm_spec if out_tm else bm_spec,
        out_shape=jax.ShapeDtypeStruct(tm_shape if out_tm else (m, d), F32),
        compiler_params=_params(("parallel",), vmem),
        name="ffn_ln",
    )(x.reshape(tm_shape) if in_tm else x, w_in, w_out, ln_g, ln_b)
    return out.reshape(m, d)


def _proj_ln_kernel(a_ref, x_ref, w_ref, g_ref, b_ref, o_ref):
    y = ALPHA * x_ref[...] + _mm(a_ref[...], w_ref[...])
    o_ref[...] = _layer_norm(y, g_ref[...], b_ref[...])


def _proj_ln(a, x, w, ln_g, ln_b):
    m, d = x.shape
    k = a.shape[1]
    tm = min(ROW_TILE, m)
    vmem = (2 * _nbytes((tm, k), BF16) + 4 * _nbytes((tm, d), F32) + _nbytes(w.shape, BF16)
            + 4 * _nbytes((tm, d), F32))
    return pl.pallas_call(
        _proj_ln_kernel,
        grid=(m // tm,),
        in_specs=[pl.BlockSpec((tm, k), lambda i: (i, 0)), pl.BlockSpec((tm, d), lambda i: (i, 0)),
                  _resident(w.shape), _resident((1, d)), _resident((1, d))],
        out_specs=pl.BlockSpec((tm, d), lambda i: (i, 0)),
        out_shape=jax.ShapeDtypeStruct((m, d), F32),
        compiler_params=_params(("parallel",), vmem),
        name="proj_ln",
    )(a, x, w, ln_g, ln_b)


def _ret_proj_kernel(x_ref, w_ref, cos_ref, sin_ref, q_ref, k_ref, v_ref, g_ref, *, heads, k_scale):
    xb = x_ref[...].astype(BF16)
    cos = cos_ref[...]
    sin = sin_ref[...]
    half = cos.shape[1]
    qk = q_ref.shape[1]
    vd = v_ref.shape[1]
    for col, dst, scale in ((0, q_ref, None), (qk, k_ref, k_scale)):
        p = _mm(xb, w_ref[:, col:col + qk])
        for h in range(heads):
            lo = 2 * half * h
            x1 = p[:, lo:lo + half]
            x2 = p[:, lo + half:lo + 2 * half]
            r1 = x1 * cos - x2 * sin
            r2 = x1 * sin + x2 * cos
            if scale is not None:
                r1 = r1 * scale
                r2 = r2 * scale
            dst[:, lo:lo + half] = r1
            dst[:, lo + half:lo + 2 * half] = r2
    v_ref[...] = _mm(xb, w_ref[:, 2 * qk:2 * qk + vd]).astype(BF16)
    g_ref[...] = _mm(xb, w_ref[:, 2 * qk + vd:])


def _ret_proj(x, w, cos, sin, qk, vd):
    m, d = x.shape
    tm = min(ROW_TILE, m)
    if cos.shape[0] < tm:
        reps = tm // cos.shape[0]
        cos, sin = jnp.tile(cos, (reps, 1)), jnp.tile(sin, (reps, 1))
    n_tab = cos.shape[0] // tm
    row = lambda n: pl.BlockSpec((tm, n), lambda i: (i, 0))
    tab = pl.BlockSpec((tm, cos.shape[1]), lambda i: (i % n_tab, 0))
    vmem = (2 * _nbytes((tm, d), F32) + _nbytes(w.shape, BF16)
            + 2 * (2 * _nbytes((tm, qk), F32) + _nbytes((tm, vd), BF16) + _nbytes((tm, vd), F32))
            + 4 * _nbytes((tm, vd), F32))
    dk = qk // RET_HEADS
    return pl.pallas_call(
        functools.partial(_ret_proj_kernel, heads=RET_HEADS, k_scale=dk ** -0.5),
        grid=(m // tm,),
        in_specs=[row(d), _resident(w.shape), tab, tab],
        out_specs=[row(qk), row(qk), row(vd), row(vd)],
        out_shape=[jax.ShapeDtypeStruct((m, qk), F32), jax.ShapeDtypeStruct((m, qk), F32),
                   jax.ShapeDtypeStruct((m, vd), BF16), jax.ShapeDtypeStruct((m, vd), F32)],
        compiler_params=_params(("parallel",), vmem),
        name="ret_proj",
    )(x, w, cos, sin)


def _ret_core_kernel(*refs, heads, chunk, zero_init):
    if zero_init:
        q_ref, k_ref, v_ref, g_ref, gn_ref, dm_ref, qd_ref, kd_ref, cd_ref, o_ref, s_ref = refs
    else:
        q_ref, k_ref, v_ref, g_ref, gn_ref, dm_ref, qd_ref, kd_ref, cd_ref, s0_ref, o_ref, s_ref = refs
    bb, tt, _ = q_ref.shape
    dk = q_ref.shape[2] // heads
    dv = v_ref.shape[2] // heads

    @pl.when(pl.program_id(1) == 0)
    def _():
        if zero_init:
            s_ref[...] = jnp.zeros(s_ref.shape, F32)
        else:
            s_ref[...] = s0_ref[...]

    for h in range(heads):
        dmask = dm_ref[h]
        q_dec = qd_ref[h]
        k_dec = kd_ref[h]
        chunk_dec = cd_ref[h]
        gn = gn_ref[:, h * dv:(h + 1) * dv]
        for b in range(bb):
            for c in range(tt // chunk):
                rows = slice(c * chunk, (c + 1) * chunk)
                qc = q_ref[b, rows, h * dk:(h + 1) * dk]
                kc = k_ref[b, rows, h * dk:(h + 1) * dk]
                vc = v_ref[b, rows, h * dv:(h + 1) * dv]
                s = s_ref[b, h]
                scores = lax.dot_general(qc.astype(BF16), kc.astype(BF16),
                                         (((1,), (1,)), ((), ())),
                                         preferred_element_type=F32) * dmask
                o = _mm(scores.astype(BF16), vc) + _mm((qc * q_dec).astype(BF16), s.astype(BF16))
                s_ref[b, h] = s * chunk_dec + lax.dot_general(
                    (kc * k_dec).astype(BF16), vc, (((0,), (0,)), ((), ())),
                    preferred_element_type=F32)
                mu = jnp.mean(o, axis=-1, keepdims=True)
                d = o - mu
                var = jnp.mean(d * d, axis=-1, keepdims=True)
                on = d * lax.rsqrt(var + GN_EPS) * gn
                gate = g_ref[b, rows, h * dv:(h + 1) * dv]
                o_ref[b, rows, h * dv:(h + 1) * dv] = (
                    gate * jax.nn.sigmoid(gate) * on).astype(BF16)


def _decay_tables(chunk):
    lg = jnp.log1p(-jnp.exp2(-5.0 - jnp.arange(RET_HEADS, dtype=F32)))
    idx = jnp.arange(chunk, dtype=F32)
    rel = idx[:, None] - idx[None, :]
    dmask = jnp.where(rel >= 0, jnp.exp(lg[:, None, None] * jnp.maximum(rel, 0.0)), 0.0)
    q_dec = jnp.exp(lg[:, None] * (idx + 1.0))[:, :, None]
    k_dec = jnp.exp(lg[:, None] * (chunk - 1.0 - idx))[:, :, None]
    chunk_dec = jnp.exp(lg * chunk)[:, None, None]
    return dmask, q_dec, k_dec, chunk_dec


def _ret_core(q, k, v, g, gn_g, s0, *, bb, tt, chunk):
    b, t, qk = q.shape
    vd = v.shape[2]
    heads = RET_HEADS
    dk, dv = qk // heads, vd // heads
    dmask, q_dec, k_dec, chunk_dec = _decay_tables(chunk)
    blk = lambda n: pl.BlockSpec((bb, tt, n), lambda i, j: (i, j, 0))
    st = pl.BlockSpec((bb, heads, dk, dv), lambda i, j: (i, 0, 0, 0))
    in_specs = [blk(qk), blk(qk), blk(vd), blk(vd), _resident((1, vd)),
                _resident(dmask.shape), _resident(q_dec.shape), _resident(k_dec.shape),
                _resident(chunk_dec.shape)]
    args = [q, k, v, g, gn_g, dmask, q_dec, k_dec, chunk_dec]
    if s0 is not None:
        in_specs.append(st)
        args.append(s0)
    n_state = 2 if s0 is None else 4
    vmem = (2 * (2 * _nbytes((bb, tt, qk), F32) + 2 * _nbytes((bb, tt, vd), BF16)
                 + _nbytes((bb, tt, vd), F32))
            + n_state * _nbytes((bb, heads, dk, dv), F32)
            + 16 * _nbytes((max(chunk, 8), dv), F32) + 8 * _nbytes((dk, dv), F32))
    return pl.pallas_call(
        functools.partial(_ret_core_kernel, heads=heads, chunk=chunk, zero_init=s0 is None),
        grid=(b // bb, t // tt),
        in_specs=in_specs,
        out_specs=[blk(vd), st],
        out_shape=[jax.ShapeDtypeStruct((b, t, vd), BF16),
                   jax.ShapeDtypeStruct((b, heads, dk, dv), F32)],
        compiler_params=_params(("parallel", "arbitrary"), vmem),
        name="ret_core",
    )(*args)


def _gelu_tanh(x):
    return 0.5 * x * (1.0 + jnp.tanh(math.sqrt(2.0 / math.pi) * (x + 0.044715 * (x * x * x))))


def _rglru_kernel(x_ref, win_ref, cw_ref, cb_ref, wbd_ref, ba_ref, bi_ref, lam_ref,
                  conv0_ref, h0_ref, wout_ref, g_ref, b_ref,
                  o_ref, convn_ref, hlast_ref,
                  xpad_ref, a_ref, u_ref, h_ref, *, nb):
    tm = x_ref.shape[0]
    hist = (CONV_W - 1) * nb
    dr = wout_ref.shape[0]
    half = dr // 2

    @pl.when(pl.program_id(0) == 0)
    def _():
        xpad_ref[0:hist, :] = conv0_ref[...]
        h_ref[...] = h0_ref[...]

    xb = x_ref[...].astype(BF16)
    gate = _gelu_tanh(_mm(xb, win_ref[:, :dr]))
    xpad_ref[hist:hist + tm, :] = _mm(xb, win_ref[:, dr:])

    conv = None
    for j in range(CONV_W):
        term = cw_ref[j:j + 1, :] * xpad_ref[j * nb:j * nb + tm, :]
        conv = term if conv is None else conv + term
    xc = cb_ref[...] + conv

    tail = xpad_ref[tm:tm + hist, :]
    convn_ref[...] = tail
    xpad_ref[0:hist, :] = tail

    lam = lam_ref[...]
    softplus_neg_lam = jnp.maximum(-lam, 0.0) + jnp.log1p(jnp.exp(-jnp.abs(lam)))
    for hf in range(2):
        cols = slice(hf * half, (hf + 1) * half)
        xh = xc[:, cols]
        z = _mm(xh.astype(BF16), wbd_ref[hf])
        r = _sigmoid(z[:, :half] + ba_ref[:, cols])
        ig = _sigmoid(z[:, half:] + bi_ref[:, cols])
        a = jnp.exp(-LRU_C * r * softplus_neg_lam[:, cols])
        a_ref[:, cols] = a
        u_ref[:, cols] = jnp.sqrt(1.0 - a * a) * (ig * xh)

    h = h_ref[...]
    for t in range(tm // nb):
        rows = slice(t * nb, (t + 1) * nb)
        h = a_ref[rows, :] * h + u_ref[rows, :]
        u_ref[rows, :] = h
    h_ref[...] = h
    hlast_ref[...] = h
    y = (gate * u_ref[...]).astype(BF16)
    o_ref[...] = _layer_norm(ALPHA * x_ref[...] + _mm(y, wout_ref[...]), g_ref[...], b_ref[...])


def _rglru(x_tm, w, conv0_tm, h0, ln_g, ln_b, *, nb):
    w_in, cw, cb, wbd, ba, bi, lam, w_out = w
    m, d = x_tm.shape
    dr = w_out.shape[0]
    tm = min(ROW_TILE, m)
    hist = (CONV_W - 1) * nb
    assert tm % nb == 0 and tm >= hist
    vmem = (4 * _nbytes((tm, d), F32) + _nbytes(w_in.shape, BF16) + _nbytes(wbd.shape, BF16)
            + _nbytes(w_out.shape, BF16) + 4 * _nbytes((hist + nb, dr), F32)
            + _nbytes((tm + hist, dr), F32) + 2 * _nbytes((tm, dr), F32) + _nbytes((nb, dr), F32)
            + 8 * _nbytes((tm, dr), F32))
    return pl.pallas_call(
        functools.partial(_rglru_kernel, nb=nb),
        grid=(m // tm,),
        in_specs=[pl.BlockSpec((tm, d), lambda i: (i, 0)),
                  _resident(w_in.shape), _resident(cw.shape),
                  _resident(cb.shape), _resident(wbd.shape), _resident(ba.shape),
                  _resident(bi.shape), _resident(lam.shape),
                  _resident(conv0_tm.shape), _resident(h0.shape),
                  _resident(w_out.shape), _resident((1, d)), _resident((1, d))],
        out_specs=[pl.BlockSpec((tm, d), lambda i: (i, 0)),
                   pl.BlockSpec((hist, dr), lambda i: (0, 0)),
                   pl.BlockSpec((nb, dr), lambda i: (0, 0))],
        out_shape=[jax.ShapeDtypeStruct((m, d), F32),
                   jax.ShapeDtypeStruct((hist, dr), F32),
                   jax.ShapeDtypeStruct((nb, dr), F32)],
        scratch_shapes=[pltpu.VMEM((tm + hist, dr), F32), pltpu.VMEM((tm, dr), F32),
                        pltpu.VMEM((tm, dr), F32), pltpu.VMEM((nb, dr), F32)],
        compiler_params=_params(("arbitrary",), vmem),
        name="rglru",
    )(x_tm, w_in, cw, cb, wbd, ba, bi, lam, conv0_tm, h0, w_out, ln_g, ln_b)


def _block_diag(blocks):
    n, bs, _ = blocks.shape
    eye = jnp.eye(n, dtype=blocks.dtype)
    return (eye[:, None, :, None] * blocks[:, :, None, :]).reshape(n * bs, n * bs)


def _prep_rec(w_in, conv_w, conv_b, w_a, b_a, w_i, b_i, lam, w_out):
    dr = w_out.shape[0]
    nblk = w_a.shape[0] // 2
    halves = [jnp.concatenate([_block_diag(w_a[s:s + nblk]), _block_diag(w_i[s:s + nblk])], axis=1)
              for s in (0, nblk)]
    wbd = jnp.stack(halves).astype(BF16)
    row = lambda v: v.reshape(1, dr)
    return (w_in.astype(BF16), conv_w, row(conv_b), wbd, row(b_a), row(b_i), row(lam),
            w_out.astype(BF16))


def _rope_tables(pos, dk):
    half = dk // 2
    inv = ROPE_BASE ** (-jnp.arange(half, dtype=F32) / half)
    ang = pos.astype(F32)[:, None] * inv[None, :]
    return jnp.cos(ang), jnp.sin(ang)


def _to_time_major(a, b, t):
    return a.reshape(b, t, a.shape[-1]).transpose(1, 0, 2).reshape(t * b, a.shape[-1])


def _to_batch_major(a, b, t):
    return a.reshape(t, b, a.shape[-1]).transpose(1, 0, 2).reshape(b * t, a.shape[-1])


def _trunk(x, pos, state_ret, state_conv, state_lru, ln_g, ln_b, ffn1, ffn2, ret, rec):
    b, t, d = x.shape
    m = b * t
    x = x.reshape(m, d)
    long_seq = t >= ROW_TILE
    ln = lambda layer, k: (ln_g[layer, k].reshape(1, d), ln_b[layer, k].reshape(1, d))
    new_ret, new_conv, new_lru = [], [], []
    for layer in range(DEPTH):
        j = layer // 2
        if layer % 2 == 0:
            x = _ffn_ln(x, ffn1[layer], *ln(layer, 0))
            w_proj, w_out, gn_g = ret[j]
            vd, qk = w_out.shape
            cos, sin = _rope_tables(pos, qk // RET_HEADS)
            chunk = math.gcd(t, RET_CHUNK)
            bb, tt = (1, ROW_TILE) if long_seq else (SHORT_SEQ_BATCH_BLOCK, t)
            q, k, v, g = _ret_proj(x, w_proj, cos, sin, qk, vd)
            s0 = None if state_ret is None else state_ret[j]
            gated, s_new = _ret_core(q.reshape(b, t, qk), k.reshape(b, t, qk), v.reshape(b, t, vd),
                                     g.reshape(b, t, vd), gn_g, s0, bb=bb, tt=tt, chunk=chunk)
            new_ret.append(s_new)
            x = _proj_ln(gated.reshape(m, vd), x, w_out, *ln(layer, 1))
            x = _ffn_ln(x, ffn2[layer], *ln(layer, 2))
        else:
            w_rec = rec[j]
            dr = w_rec[-1].shape[0]
            if state_conv is None:
                conv0 = jnp.zeros(((CONV_W - 1) * b, dr), F32)
                h0 = jnp.zeros((b, dr), F32)
            else:
                conv0 = state_conv[j].transpose(1, 0, 2).reshape((CONV_W - 1) * b, dr)
                h0 = state_lru[j]
            x_tm = _ffn_ln(x, ffn1[layer], *ln(layer, 0), out_tm=long_seq, nb=b)
            if not long_seq:
                x_tm = _to_time_major(x_tm, b, t)
            x_tm, conv_tm, h_last = _rglru(x_tm, w_rec, conv0, h0, *ln(layer, 1), nb=b)
            new_conv.append(conv_tm.reshape(CONV_W - 1, b, dr).transpose(1, 0, 2))
            new_lru.append(h_last)
            x = _ffn_ln(x_tm, ffn2[layer], *ln(layer, 2), in_tm=long_seq, nb=b)
            if not long_seq:
                x = _to_batch_major(x, b, t)
    return x.reshape(b, t, d), jnp.stack(new_ret), jnp.stack(new_conv), jnp.stack(new_lru)


def kernel(x_prompt, x_sample, state_ret, state_conv, state_lru, ln_g, ln_b, ffn1_w_in, ffn1_w_out,
           ffn2_w_in, ffn2_w_out, ret_w_in, ret_gn_g, ret_w_out, rec_w_in, rec_conv_w, rec_conv_b,
           rec_w_a, rec_b_a, rec_w_i, rec_b_i, rec_lam, rec_w_out):
    cast = lambda w: w.astype(BF16)
    ffn1 = [(cast(ffn1_w_in[l]), cast(ffn1_w_out[l])) for l in range(DEPTH)]
    ffn2 = [(cast(ffn2_w_in[l]), cast(ffn2_w_out[l])) for l in range(DEPTH)]
    ret = [(cast(ret_w_in[j]), cast(ret_w_out[j]), ret_gn_g[j].reshape(1, -1))
           for j in range(ret_w_in.shape[0])]
    rec = [_prep_rec(rec_w_in[j], rec_conv_w[j], rec_conv_b[j], rec_w_a[j], rec_b_a[j], rec_w_i[j],
                     rec_b_i[j], rec_lam[j], rec_w_out[j]) for j in range(rec_w_in.shape[0])]
    weights = (ln_g, ln_b, ffn1, ffn2, ret, rec)
    tp, ts = x_prompt.shape[1], x_sample.shape[1]
    y_p, ret_p, conv_p, lru_p = _trunk(x_prompt, jnp.arange(tp, dtype=jnp.int32),
                                       None, None, None, *weights)
    y_s, ret_s, conv_s, lru_s = _trunk(x_sample, PAST_LEN + jnp.arange(ts, dtype=jnp.int32),
                                       state_ret, state_conv, state_lru, *weights)
    return (y_p, y_s, ret_p, conv_p, lru_p, ret_s, conv_s, lru_s)
```

```python
import functools
import math

import numpy as np
import jax
import jax.numpy as jnp
from jax import lax
from jax.experimental import pallas as pl
from jax.experimental.pallas import tpu as pltpu

F32 = jnp.float32
BF16 = jnp.bfloat16

DEPTH = 2
PAST_LEN = 16384
RET_HEADS = 4
RET_CHUNK = 128
ROPE_BASE = 10000.0
CONV_W = 4
LRU_C = 8.0
ALPHA = (2.0 * DEPTH) ** 0.25
LN_EPS = 1e-5
GN_EPS = 1e-6

V7X_SCOPED_VMEM_BYTES = 60000 * 1024
LANES = 128
SUBLANES = 8
ROW_TILE = 512
FF_CHUNK = 256
SHORT_SEQ_BATCH_BLOCK = 4


def _params(semantics, vmem_bytes):
    return pltpu.CompilerParams(
        dimension_semantics=semantics,
        vmem_limit_bytes=int(min(V7X_SCOPED_VMEM_BYTES, vmem_bytes)),
    )


def _resident(shape):
    zeros = (0,) * len(shape)
    return pl.BlockSpec(shape, lambda *_: zeros, pipeline_mode=pl.Buffered(1))


def _resident_slice(shape, idx):
    index = (idx,) + (0,) * (len(shape) - 1)
    return pl.BlockSpec((None,) + tuple(shape[1:]), lambda *_: index, pipeline_mode=pl.Buffered(1))


def _nbytes(shape, dtype):
    return int(np.prod(shape)) * jnp.dtype(dtype).itemsize


def _layer_norm(y, g, b):
    mu = jnp.mean(y, axis=-1, keepdims=True)
    d = y - mu
    var = jnp.mean(d * d, axis=-1, keepdims=True)
    return d * lax.rsqrt(var + LN_EPS) * g + b


def _mm(a, b):
    return jnp.dot(a, b, preferred_element_type=F32)


def _sigmoid(x):
    return 0.5 * jnp.tanh(0.5 * x) + 0.5


def _ffn_ln_kernel(x_ref, win_ref, wout_ref, g_ref, b_ref, o_ref, *, ff):
    x = x_ref[...]
    xb = x.astype(BF16)
    acc = jnp.zeros(x.shape, F32)
    for lo in range(0, ff, FF_CHUNK):
        gate = _mm(xb, win_ref[:, lo:lo + FF_CHUNK])
        up = _mm(xb, win_ref[:, ff + lo:ff + lo + FF_CHUNK])
        h = (gate * jax.nn.sigmoid(gate) * up).astype(BF16)
        acc = acc + _mm(h, wout_ref[lo:lo + FF_CHUNK, :])
    o_ref[...] = _layer_norm(ALPHA * x + 0.5 * acc, g_ref[...], b_ref[...])


def _ffn_ln(x, w, layer, ln, ln_idx):
    w_in, w_out = w
    ln_g, ln_b = ln
    m, d = x.shape
    ff = w_out.shape[1]
    assert ff % FF_CHUNK == 0
    tm = min(ROW_TILE, m)
    row = pl.BlockSpec((tm, d), lambda i: (i, 0))
    vmem = (4 * _nbytes((tm, d), F32) + _nbytes(w_in.shape[1:], BF16)
            + _nbytes(w_out.shape[1:], BF16) + 8 * _nbytes((tm, d), F32))
    return pl.pallas_call(
        functools.partial(_ffn_ln_kernel, ff=ff),
        grid=(m // tm,),
        in_specs=[row, _resident_slice(w_in.shape, layer), _resident_slice(w_out.shape, layer),
                  _resident_slice(ln_g.shape, ln_idx), _resident_slice(ln_b.shape, ln_idx)],
        out_specs=row,
        out_shape=jax.ShapeDtypeStruct((m, d), F32),
        compiler_params=_params(("parallel",), vmem),
        name="ffn_ln",
    )(x, w_in, w_out, ln_g, ln_b)


def _proj_ln_kernel(a_ref, x_ref, w_ref, g_ref, b_ref, o_ref):
    y = ALPHA * x_ref[...] + _mm(a_ref[...], w_ref[...])
    o_ref[...] = _layer_norm(y, g_ref[...], b_ref[...])


def _proj_ln(a, x, w, j, ln, ln_idx):
    ln_g, ln_b = ln
    m, d = x.shape
    k = a.shape[1]
    tm = min(ROW_TILE, m)
    vmem = (2 * _nbytes((tm, k), BF16) + 4 * _nbytes((tm, d), F32) + _nbytes(w.shape[1:], BF16)
            + 4 * _nbytes((tm, d), F32))
    return pl.pallas_call(
        _proj_ln_kernel,
        grid=(m // tm,),
        in_specs=[pl.BlockSpec((tm, k), lambda i: (i, 0)), pl.BlockSpec((tm, d), lambda i: (i, 0)),
                  _resident_slice(w.shape, j),
                  _resident_slice(ln_g.shape, ln_idx), _resident_slice(ln_b.shape, ln_idx)],
        out_specs=pl.BlockSpec((tm, d), lambda i: (i, 0)),
        out_shape=jax.ShapeDtypeStruct((m, d), F32),
        compiler_params=_params(("parallel",), vmem),
        name="proj_ln",
    )(a, x, w, ln_g, ln_b)


def _ret_proj_kernel(x_ref, w_ref, cos_ref, sin_ref, q_ref, k_ref, v_ref, g_ref, *, heads, k_scale):
    xb = x_ref[...].astype(BF16)
    cos = cos_ref[...]
    sin = sin_ref[...]
    half = cos.shape[1]
    qk = q_ref.shape[1]
    vd = v_ref.shape[1]
    for col, dst, scale in ((0, q_ref, None), (qk, k_ref, k_scale)):
        p = _mm(xb, w_ref[:, col:col + qk])
        for h in range(heads):
            lo = 2 * half * h
            x1 = p[:, lo:lo + half]
            x2 = p[:, lo + half:lo + 2 * half]
            r1 = x1 * cos - x2 * sin
            r2 = x1 * sin + x2 * cos
            if scale is not None:
                r1 = r1 * scale
                r2 = r2 * scale
            dst[:, lo:lo + half] = r1
            dst[:, lo + half:lo + 2 * half] = r2
    v_ref[...] = _mm(xb, w_ref[:, 2 * qk:2 * qk + vd]).astype(BF16)
    g_ref[...] = _mm(xb, w_ref[:, 2 * qk + vd:])


def _ret_proj(x, w, j, cos, sin, qk, vd):
    m, d = x.shape
    tm = min(ROW_TILE, m)
    if cos.shape[0] < tm:
        reps = tm // cos.shape[0]
        cos, sin = jnp.tile(cos, (reps, 1)), jnp.tile(sin, (reps, 1))
    n_tab = cos.shape[0] // tm
    row = lambda n: pl.BlockSpec((tm, n), lambda i: (i, 0))
    tab = pl.BlockSpec((tm, cos.shape[1]), lambda i: (i % n_tab, 0))
    vmem = (2 * _nbytes((tm, d), F32) + _nbytes(w.shape[1:], BF16)
            + 2 * (2 * _nbytes((tm, qk), F32) + _nbytes((tm, vd), BF16) + _nbytes((tm, vd), F32))
            + 4 * _nbytes((tm, vd), F32))
    dk = qk // RET_HEADS
    return pl.pallas_call(
        functools.partial(_ret_proj_kernel, heads=RET_HEADS, k_scale=dk ** -0.5),
        grid=(m // tm,),
        in_specs=[row(d), _resident_slice(w.shape, j), tab, tab],
        out_specs=[row(qk), row(qk), row(vd), row(vd)],
        out_shape=[jax.ShapeDtypeStruct((m, qk), F32), jax.ShapeDtypeStruct((m, qk), F32),
                   jax.ShapeDtypeStruct((m, vd), BF16), jax.ShapeDtypeStruct((m, vd), F32)],
        compiler_params=_params(("parallel",), vmem),
        name="ret_proj",
    )(x, w, cos, sin)


def _ret_core_kernel(*refs, heads, chunk, zero_init):
    if zero_init:
        q_ref, k_ref, v_ref, g_ref, gn_ref, dm_ref, qd_ref, kd_ref, cd_ref, o_ref, s_ref = refs
    else:
        q_ref, k_ref, v_ref, g_ref, gn_ref, dm_ref, qd_ref, kd_ref, cd_ref, s0_ref, o_ref, s_ref = refs
    bb, tt, _ = q_ref.shape
    dk = q_ref.shape[2] // heads
    dv = v_ref.shape[2] // heads

    @pl.when(pl.program_id(1) == 0)
    def _():
        if zero_init:
            s_ref[...] = jnp.zeros(s_ref.shape, F32)
        else:
            s_ref[...] = s0_ref[...]

    for h in range(heads):
        dmask = dm_ref[h]
        q_dec = qd_ref[h]
        k_dec = kd_ref[h]
        chunk_dec = cd_ref[h]
        gn = gn_ref[:, h * dv:(h + 1) * dv]
        for b in range(bb):
            for c in range(tt // chunk):
                rows = slice(c * chunk, (c + 1) * chunk)
                qc = q_ref[b, rows, h * dk:(h + 1) * dk]
                kc = k_ref[b, rows, h * dk:(h + 1) * dk]
                vc = v_ref[b, rows, h * dv:(h + 1) * dv]
                s = s_ref[b, h]
                scores = lax.dot_general(qc.astype(BF16), kc.astype(BF16),
                                         (((1,), (1,)), ((), ())),
                                         preferred_element_type=F32) * dmask
                o = _mm(scores.astype(BF16), vc) + _mm((qc * q_dec).astype(BF16), s.astype(BF16))
                s_ref[b, h] = s * chunk_dec + lax.dot_general(
                    (kc * k_dec).astype(BF16), vc, (((0,), (0,)), ((), ())),
                    preferred_element_type=F32)
                mu = jnp.mean(o, axis=-1, keepdims=True)
                d = o - mu
                var = jnp.mean(d * d, axis=-1, keepdims=True)
                on = d * lax.rsqrt(var + GN_EPS) * gn
                gate = g_ref[b, rows, h * dv:(h + 1) * dv]
                o_ref[b, rows, h * dv:(h + 1) * dv] = (
                    gate * jax.nn.sigmoid(gate) * on).astype(BF16)


def _decay_tables(chunk):
    lg = jnp.log1p(-jnp.exp2(-5.0 - jnp.arange(RET_HEADS, dtype=F32)))
    idx = jnp.arange(chunk, dtype=F32)
    rel = idx[:, None] - idx[None, :]
    dmask = jnp.where(rel >= 0, jnp.exp(lg[:, None, None] * jnp.maximum(rel, 0.0)), 0.0)
    q_dec = jnp.exp(lg[:, None] * (idx + 1.0))[:, :, None]
    k_dec = jnp.exp(lg[:, None] * (chunk - 1.0 - idx))[:, :, None]
    chunk_dec = jnp.exp(lg * chunk)[:, None, None]
    return dmask, q_dec, k_dec, chunk_dec


def _ret_core(q, k, v, g, gn_g, j, s0, *, bb, tt, chunk):
    b, t, qk = q.shape
    vd = v.shape[2]
    heads = RET_HEADS
    dk, dv = qk // heads, vd // heads
    dmask, q_dec, k_dec, chunk_dec = _decay_tables(chunk)
    blk = lambda n: pl.BlockSpec((bb, tt, n), lambda i, jj: (i, jj, 0))
    st_out = pl.BlockSpec((bb, heads, dk, dv), lambda i, jj: (i, 0, 0, 0))
    in_specs = [blk(qk), blk(qk), blk(vd), blk(vd), _resident_slice(gn_g.shape, j),
                _resident(dmask.shape), _resident(q_dec.shape), _resident(k_dec.shape),
                _resident(chunk_dec.shape)]
    args = [q, k, v, g, gn_g, dmask, q_dec, k_dec, chunk_dec]
    if s0 is not None:
        in_specs.append(pl.BlockSpec((None, bb, heads, dk, dv), lambda i, jj: (j, i, 0, 0, 0)))
        args.append(s0)
    n_state = 2 if s0 is None else 4
    vmem = (2 * (2 * _nbytes((bb, tt, qk), F32) + 2 * _nbytes((bb, tt, vd), BF16)
                 + _nbytes((bb, tt, vd), F32))
            + n_state * _nbytes((bb, heads, dk, dv), F32)
            + 16 * _nbytes((max(chunk, SUBLANES), dv), F32) + 8 * _nbytes((dk, dv), F32))
    return pl.pallas_call(
        functools.partial(_ret_core_kernel, heads=heads, chunk=chunk, zero_init=s0 is None),
        grid=(b // bb, t // tt),
        in_specs=in_specs,
        out_specs=[blk(vd), st_out],
        out_shape=[jax.ShapeDtypeStruct((b, t, vd), BF16),
                   jax.ShapeDtypeStruct((b, heads, dk, dv), F32)],
        compiler_params=_params(("parallel", "arbitrary"), vmem),
        name="ret_core",
    )(*args)


def _gelu_tanh(x):
    return 0.5 * x * (1.0 + jnp.tanh(math.sqrt(2.0 / math.pi) * (x + 0.044715 * (x * x * x))))


def _rglru_kernel(x_ref, win_ref, cw_ref, cb_ref, wbd_ref, ba_ref, bi_ref, lam_ref,
                  conv0_ref, h0_ref, wout_ref, g_ref, b_ref,
                  o_ref, convn_ref, hlast_ref,
                  xpad_ref, a_ref, u_ref, h_ref, *, nb):
    batch_major = len(x_ref.shape) == 3
    dr = wout_ref.shape[0]
    n_slab = dr // LANES
    half = dr // 2
    hist = (CONV_W - 1) * nb
    tm = a_ref.shape[1]
    nt = tm // nb
    lanes = lambda s: slice(s * LANES, (s + 1) * LANES)

    @pl.when(pl.program_id(0) == 0)
    def _():
        xpad_ref[:, 0:hist, :] = conv0_ref[...]
        h_ref[...] = h0_ref[...]

    x = x_ref[...].reshape(tm, x_ref.shape[-1])
    xb = x.astype(BF16)
    gate = _gelu_tanh(_mm(xb, win_ref[:, :dr]))
    conv_in = _mm(xb, win_ref[:, dr:])
    for s in range(n_slab):
        if batch_major:
            for b in range(nb):
                xpad_ref[s, pl.ds(hist + b, nt, stride=nb), :] = conv_in[b * nt:(b + 1) * nt, lanes(s)]
        else:
            xpad_ref[s, hist:hist + tm, :] = conv_in[:, lanes(s)]

    conv = None
    for j in range(CONV_W):
        term = cw_ref[j] * xpad_ref[:, j * nb:j * nb + tm, :]
        conv = term if conv is None else conv + term
    conv = cb_ref[...] + conv
    xc = jnp.concatenate([conv[s] for s in range(n_slab)], axis=1)

    tail = xpad_ref[:, tm:tm + hist, :]
    convn_ref[...] = tail
    xpad_ref[:, 0:hist, :] = tail

    lam = lam_ref[...]
    softplus_neg_lam = jnp.maximum(-lam, 0.0) + jnp.log1p(jnp.exp(-jnp.abs(lam)))
    for hf in range(2):
        cols = slice(hf * half, (hf + 1) * half)
        xh = xc[:, cols]
        z = _mm(xh.astype(BF16), wbd_ref[hf])
        r = _sigmoid(z[:, :half] + ba_ref[:, cols])
        ig = _sigmoid(z[:, half:] + bi_ref[:, cols])
        a = jnp.exp(-LRU_C * r * softplus_neg_lam[:, cols])
        u = jnp.sqrt(1.0 - a * a) * (ig * xh)
        for s in range(half // LANES):
            a_ref[hf * (half // LANES) + s] = a[:, lanes(s)]
            u_ref[hf * (half // LANES) + s] = u[:, lanes(s)]

    h = h_ref[...]
    for t in range(nt):
        rows = slice(t * nb, (t + 1) * nb)
        h = a_ref[:, rows, :] * h + u_ref[:, rows, :]
        u_ref[:, rows, :] = h
    h_ref[...] = h
    hlast_ref[...] = h

    if batch_major:
        hs = jnp.concatenate(
            [jnp.concatenate([u_ref[s, pl.ds(b, nt, stride=nb), :] for b in range(nb)], axis=0)
             for s in range(n_slab)], axis=1)
    else:
        hs = jnp.concatenate([u_ref[s] for s in range(n_slab)], axis=1)
    y = (gate * hs).astype(BF16)
    out = _layer_norm(ALPHA * x + _mm(y, wout_ref[...]), g_ref[...], b_ref[...])
    o_ref[...] = out.reshape(o_ref.shape)


def _rglru(x, w, j, conv0, h0, ln, ln_idx, *, nb):
    w_in, cw, cb, wbd, ba, bi, lam, w_out = w
    ln_g, ln_b = ln
    dr, d = w_out.shape[1:]
    n_slab = dr // LANES
    hist = (CONV_W - 1) * nb
    if x.ndim == 3:
        nt = ROW_TILE // nb
        tm = ROW_TILE
        steps = x.shape[1] // nt
        x_spec = pl.BlockSpec((nb, nt, d), lambda i: (0, i, 0))
    else:
        tm = min(ROW_TILE, x.shape[0])
        steps = x.shape[0] // tm
        x_spec = pl.BlockSpec((tm, d), lambda i: (i, 0))
    assert tm % nb == 0 and tm >= hist and nb % SUBLANES == 0
    vmem = (4 * _nbytes((tm, d), F32) + _nbytes(w_in.shape[1:], BF16) + _nbytes(wbd.shape[1:], BF16)
            + _nbytes(w_out.shape[1:], BF16) + 4 * _nbytes((hist + nb, dr), F32)
            + _nbytes((tm + hist, dr), F32) + 2 * _nbytes((tm, dr), F32) + _nbytes((nb, dr), F32)
            + 10 * _nbytes((tm, dr), F32))
    sl = _resident_slice
    return pl.pallas_call(
        functools.partial(_rglru_kernel, nb=nb),
        grid=(steps,),
        in_specs=[x_spec, sl(w_in.shape, j), sl(cw.shape, j), sl(cb.shape, j), sl(wbd.shape, j),
                  sl(ba.shape, j), sl(bi.shape, j), sl(lam.shape, j),
                  _resident(conv0.shape), _resident(h0.shape),
                  sl(w_out.shape, j), sl(ln_g.shape, ln_idx), sl(ln_b.shape, ln_idx)],
        out_specs=[x_spec,
                   pl.BlockSpec((n_slab, hist, LANES), lambda i: (0, 0, 0)),
                   pl.BlockSpec((n_slab, nb, LANES), lambda i: (0, 0, 0))],
        out_shape=[jax.ShapeDtypeStruct(x.shape, F32),
                   jax.ShapeDtypeStruct((n_slab, hist, LANES), F32),
                   jax.ShapeDtypeStruct((n_slab, nb, LANES), F32)],
        scratch_shapes=[pltpu.VMEM((n_slab, tm + hist, LANES), F32),
                        pltpu.VMEM((n_slab, tm, LANES), F32),
                        pltpu.VMEM((n_slab, tm, LANES), F32),
                        pltpu.VMEM((n_slab, nb, LANES), F32)],
        compiler_params=_params(("arbitrary",), vmem),
        name="rglru",
    )(x, w_in, cw, cb, wbd, ba, bi, lam, conv0, h0, w_out, ln_g, ln_b)


def _block_diag(blocks):
    n, bs, _ = blocks.shape
    eye = jnp.eye(n, dtype=blocks.dtype)
    return (eye[:, None, :, None] * blocks[:, :, None, :]).reshape(n * bs, n * bs)


def _prep_rec(w_in, conv_w, conv_b, w_a, b_a, w_i, b_i, lam, w_out):
    n_layers, dr, _ = w_out.shape
    n_slab = dr // LANES
    nblk = w_a.shape[1] // 2
    wbd = jnp.stack([
        jnp.stack([jnp.concatenate([_block_diag(w_a[j, s:s + nblk]), _block_diag(w_i[j, s:s + nblk])],
                                   axis=1) for s in (0, nblk)])
        for j in range(n_layers)]).astype(BF16)
    row = lambda v: v.reshape(n_layers, 1, dr)
    return (w_in.astype(BF16), conv_w.reshape(n_layers, CONV_W, n_slab, 1, LANES),
            conv_b.reshape(n_layers, n_slab, 1, LANES), wbd, row(b_a), row(b_i), row(lam),
            w_out.astype(BF16))


def _to_slabs(a):
    rows, dr = a.shape
    return a.reshape(rows, dr // LANES, LANES).transpose(1, 0, 2)


def _from_slabs(a):
    n_slab, rows, _ = a.shape
    return a.transpose(1, 0, 2).reshape(rows, n_slab * LANES)


def _rope_tables(pos, dk):
    half = dk // 2
    inv = ROPE_BASE ** (-jnp.arange(half, dtype=F32) / half)
    ang = pos.astype(F32)[:, None] * inv[None, :]
    return jnp.cos(ang), jnp.sin(ang)


def _to_time_major(a, b, t):
    return a.reshape(b, t, a.shape[-1]).transpose(1, 0, 2).reshape(t * b, a.shape[-1])


def _to_batch_major(a, b, t):
    return a.reshape(t, b, a.shape[-1]).transpose(1, 0, 2).reshape(b * t, a.shape[-1])


def _trunk(x, pos, state_ret, state_conv, state_lru, ln, ffn1, ffn2, ret, rec):
    b, t, d = x.shape
    m = b * t
    x = x.reshape(m, d)
    long_seq = t >= ROW_TILE
    new_ret, new_conv, new_lru = [], [], []
    for layer in range(DEPTH):
        j = layer // 2
        x = _ffn_ln(x, ffn1, layer, ln, 3 * layer)
        if layer % 2 == 0:
            w_proj, w_out, gn_g = ret
            vd, qk = w_out.shape[1:]
            cos, sin = _rope_tables(pos, qk // RET_HEADS)
            chunk = math.gcd(t, RET_CHUNK)
            bb, tt = (1, ROW_TILE) if long_seq else (SHORT_SEQ_BATCH_BLOCK, t)
            q, k, v, g = _ret_proj(x, w_proj, j, cos, sin, qk, vd)
            gated, s_new = _ret_core(q.reshape(b, t, qk), k.reshape(b, t, qk), v.reshape(b, t, vd),
                                     g.reshape(b, t, vd), gn_g, j, state_ret,
                                     bb=bb, tt=tt, chunk=chunk)
            new_ret.append(s_new)
            x = _proj_ln(gated.reshape(m, vd), x, w_out, j, ln, 3 * layer + 1)
        else:
            dr = rec[-1].shape[1]
            if state_conv is None:
                conv0 = jnp.zeros((dr // LANES, (CONV_W - 1) * b, LANES), F32)
                h0 = jnp.zeros((dr // LANES, b, LANES), F32)
            else:
                conv0 = _to_slabs(state_conv[j].transpose(1, 0, 2).reshape((CONV_W - 1) * b, dr))
                h0 = _to_slabs(state_lru[j])
            x_in = x.reshape(b, t, d) if long_seq else _to_time_major(x, b, t)
            x_out, conv_n, h_last = _rglru(x_in, rec, j, conv0, h0, ln, 3 * layer + 1, nb=b)
            x = x_out.reshape(m, d) if long_seq else _to_batch_major(x_out, b, t)
            new_conv.append(_from_slabs(conv_n).reshape(CONV_W - 1, b, dr).transpose(1, 0, 2))
            new_lru.append(_from_slabs(h_last))
        x = _ffn_ln(x, ffn2, layer, ln, 3 * layer + 2)
    return x.reshape(b, t, d), jnp.stack(new_ret), jnp.stack(new_conv), jnp.stack(new_lru)


def kernel(x_prompt, x_sample, state_ret, state_conv, state_lru, ln_g, ln_b, ffn1_w_in, ffn1_w_out,
           ffn2_w_in, ffn2_w_out, ret_w_in, ret_gn_g, ret_w_out, rec_w_in, rec_conv_w, rec_conv_b,
           rec_w_a, rec_b_a, rec_w_i, rec_b_i, rec_lam, rec_w_out):
    cast = lambda w: w.astype(BF16)
    d = ln_g.shape[-1]
    ln = (ln_g.reshape(-1, 1, d), ln_b.reshape(-1, 1, d))
    ffn1 = (cast(ffn1_w_in), cast(ffn1_w_out))
    ffn2 = (cast(ffn2_w_in), cast(ffn2_w_out))
    ret = (cast(ret_w_in), cast(ret_w_out), ret_gn_g.reshape(ret_gn_g.shape[0], 1, -1))
    rec = _prep_rec(rec_w_in, rec_conv_w, rec_conv_b, rec_w_a, rec_b_a, rec_w_i, rec_b_i, rec_lam,
                    rec_w_out)
    weights = (ln, ffn1, ffn2, ret, rec)
    tp, ts = x_prompt.shape[1], x_sample.shape[1]
    y_p, ret_p, conv_p, lru_p = _trunk(x_prompt, jnp.arange(tp, dtype=jnp.int32),
                                       None, None, None, *weights)
    y_s, ret_s, conv_s, lru_s = _trunk(x_sample, PAST_LEN + jnp.arange(ts, dtype=jnp.int32),
                                       state_ret, state_conv, state_lru, *weights)
    return (y_p, y_s, ret_p, conv_p, lru_p, ret_s, conv_s, lru_s)
```

```python
import functools
import math

import numpy as np
import jax
import jax.numpy as jnp
from jax import lax
from jax.experimental import pallas as pl
from jax.experimental.pallas import tpu as pltpu

F32 = jnp.float32
BF16 = jnp.bfloat16

DEPTH = 2
PAST_LEN = 16384
RET_HEADS = 4
RET_CHUNK = 128
ROPE_BASE = 10000.0
CONV_W = 4
LRU_C = 8.0
ALPHA = (2.0 * DEPTH) ** 0.25
LN_EPS = 1e-5
GN_EPS = 1e-6
RSQRT_FLOOR = 1e-30

V7X_SCOPED_VMEM_BYTES = 60000 * 1024
LANES = 128
SUBLANES = 8
ROW_TILE = 512
FF_CHUNK = 256
PROJ_CHUNK = 256
GATE_COLS = 256
GATE_ROWS = 512
SHORT_SEQ_BATCH_BLOCK = 4


def _params(semantics, vmem_bytes):
    return pltpu.CompilerParams(
        dimension_semantics=semantics,
        vmem_limit_bytes=int(min(V7X_SCOPED_VMEM_BYTES, vmem_bytes)),
    )


def _resident(shape):
    zeros = (0,) * len(shape)
    return pl.BlockSpec(shape, lambda *_: zeros, pipeline_mode=pl.Buffered(1))


def _resident_slice(shape, idx):
    index = (idx,) + (0,) * (len(shape) - 1)
    return pl.BlockSpec((None,) + tuple(shape[1:]), lambda *_: index, pipeline_mode=pl.Buffered(1))


def _nbytes(shape, dtype):
    return int(np.prod(shape)) * jnp.dtype(dtype).itemsize


def _layer_norm(y, g, b):
    mu = jnp.mean(y, axis=-1, keepdims=True)
    d = y - mu
    var = jnp.mean(d * d, axis=-1, keepdims=True)
    return d * lax.rsqrt(var + LN_EPS) * g + b


def _mm(a, b):
    return jnp.dot(a, b, preferred_element_type=F32)


def _ffn_ln_kernel(x_ref, win_ref, wout_ref, g_ref, b_ref, o_ref, *, ff):
    x = x_ref[...]
    xb = x.astype(BF16)
    acc = jnp.zeros(x.shape, F32)
    for lo in range(0, ff, FF_CHUNK):
        gate = _mm(xb, win_ref[:, lo:lo + FF_CHUNK])
        up = _mm(xb, win_ref[:, ff + lo:ff + lo + FF_CHUNK])
        h = (gate * jax.nn.sigmoid(gate) * up).astype(BF16)
        acc = acc + _mm(h, wout_ref[lo:lo + FF_CHUNK, :])
    o_ref[...] = _layer_norm(ALPHA * x + 0.5 * acc, g_ref[...], b_ref[...])


def _ffn_ln(x, w, layer, ln, ln_idx):
    w_in, w_out = w
    ln_g, ln_b = ln
    m, d = x.shape
    ff = w_out.shape[1]
    assert ff % FF_CHUNK == 0
    tm = min(ROW_TILE, m)
    row = pl.BlockSpec((tm, d), lambda i: (i, 0))
    vmem = (4 * _nbytes((tm, d), F32) + _nbytes(w_in.shape[1:], BF16)
            + _nbytes(w_out.shape[1:], BF16) + 8 * _nbytes((tm, d), F32))
    return pl.pallas_call(
        functools.partial(_ffn_ln_kernel, ff=ff),
        grid=(m // tm,),
        in_specs=[row, _resident_slice(w_in.shape, layer), _resident_slice(w_out.shape, layer),
                  _resident_slice(ln_g.shape, ln_idx), _resident_slice(ln_b.shape, ln_idx)],
        out_specs=row,
        out_shape=jax.ShapeDtypeStruct((m, d), F32),
        compiler_params=_params(("parallel",), vmem),
        name="ffn_ln",
    )(x, w_in, w_out, ln_g, ln_b)


def _proj_ln_kernel(a_ref, x_ref, w_ref, g_ref, b_ref, o_ref):
    y = ALPHA * x_ref[...] + _mm(a_ref[...], w_ref[...])
    o_ref[...] = _layer_norm(y, g_ref[...], b_ref[...])


def _proj_ln(a, x, w, j, ln, ln_idx):
    ln_g, ln_b = ln
    m, d = x.shape
    k = a.shape[1]
    tm = min(ROW_TILE, m)
    vmem = (2 * _nbytes((tm, k), BF16) + 4 * _nbytes((tm, d), F32) + _nbytes(w.shape[1:], BF16)
            + 4 * _nbytes((tm, d), F32))
    return pl.pallas_call(
        _proj_ln_kernel,
        grid=(m // tm,),
        in_specs=[pl.BlockSpec((tm, k), lambda i: (i, 0)), pl.BlockSpec((tm, d), lambda i: (i, 0)),
                  _resident_slice(w.shape, j),
                  _resident_slice(ln_g.shape, ln_idx), _resident_slice(ln_b.shape, ln_idx)],
        out_specs=pl.BlockSpec((tm, d), lambda i: (i, 0)),
        out_shape=jax.ShapeDtypeStruct((m, d), F32),
        compiler_params=_params(("parallel",), vmem),
        name="proj_ln",
    )(a, x, w, ln_g, ln_b)


def _ret_proj_kernel(x_ref, w_ref, cos_ref, sin_ref, q_ref, k_ref, v_ref, g_ref, *, heads, k_scale):
    xb = x_ref[...].astype(BF16)
    cos = cos_ref[...]
    sin = sin_ref[...]
    half = cos.shape[1]
    qk = q_ref.shape[1]
    vd = v_ref.shape[1]
    for col, dst, scale in ((0, q_ref, None), (qk, k_ref, k_scale)):
        p = _mm(xb, w_ref[:, col:col + qk])
        for h in range(heads):
            lo = 2 * half * h
            x1 = p[:, lo:lo + half]
            x2 = p[:, lo + half:lo + 2 * half]
            r1 = x1 * cos - x2 * sin
            r2 = x1 * sin + x2 * cos
            if scale is not None:
                r1 = r1 * scale
                r2 = r2 * scale
            dst[:, lo:lo + half] = r1.astype(BF16)
            dst[:, lo + half:lo + 2 * half] = r2.astype(BF16)
    v_ref[...] = _mm(xb, w_ref[:, 2 * qk:2 * qk + vd]).astype(BF16)
    g_ref[...] = _mm(xb, w_ref[:, 2 * qk + vd:]).astype(BF16)


def _ret_proj(x, w, j, cos, sin, qk, vd):
    m, d = x.shape
    tm = min(ROW_TILE, m)
    if cos.shape[0] < tm:
        reps = tm // cos.shape[0]
        cos, sin = jnp.tile(cos, (reps, 1)), jnp.tile(sin, (reps, 1))
    n_tab = cos.shape[0] // tm
    row = lambda n: pl.BlockSpec((tm, n), lambda i: (i, 0))
    tab = pl.BlockSpec((tm, cos.shape[1]), lambda i: (i % n_tab, 0))
    vmem = (2 * _nbytes((tm, d), F32) + _nbytes(w.shape[1:], BF16)
            + 4 * (_nbytes((tm, qk), BF16) + _nbytes((tm, vd), BF16))
            + 4 * _nbytes((tm, vd), F32))
    dk = qk // RET_HEADS
    return pl.pallas_call(
        functools.partial(_ret_proj_kernel, heads=RET_HEADS, k_scale=dk ** -0.5),
        grid=(m // tm,),
        in_specs=[row(d), _resident_slice(w.shape, j), tab, tab],
        out_specs=[row(qk), row(qk), row(vd), row(vd)],
        out_shape=[jax.ShapeDtypeStruct((m, qk), BF16), jax.ShapeDtypeStruct((m, qk), BF16),
                   jax.ShapeDtypeStruct((m, vd), BF16), jax.ShapeDtypeStruct((m, vd), BF16)],
        compiler_params=_params(("parallel",), vmem),
        name="ret_proj",
    )(x, w, cos, sin)


def _ret_core_kernel(*refs, heads, chunk, zero_init, fuse_out):
    refs = list(refs)
    q_ref, k_ref, v_ref, g_ref, gn_ref, dm_ref, qd_ref, kd_ref, cd_ref = refs[:9]
    del refs[:9]
    s0_ref = None if zero_init else refs.pop(0)
    if fuse_out:
        x_ref, wout_ref, lng_ref, lnb_ref, o_ref, s_ref, gated_ref = refs
    else:
        o_ref, s_ref = refs
        gated_ref = o_ref
    bb, tt, _ = q_ref.shape
    dk = q_ref.shape[2] // heads
    dv = v_ref.shape[2] // heads

    @pl.when(pl.program_id(1) == 0)
    def _():
        if zero_init:
            s_ref[...] = jnp.zeros(s_ref.shape, F32)
        else:
            s_ref[...] = s0_ref[...]

    acc = None
    for h in range(heads):
        dmask = dm_ref[h]
        q_dec = qd_ref[h]
        k_dec = kd_ref[h]
        chunk_dec = cd_ref[h]
        gn = gn_ref[:, h * dv:(h + 1) * dv]
        for b in range(bb):
            for c in range(tt // chunk):
                rows = slice(c * chunk, (c + 1) * chunk)
                qc = q_ref[b, rows, h * dk:(h + 1) * dk]
                kc = k_ref[b, rows, h * dk:(h + 1) * dk]
                vc = v_ref[b, rows, h * dv:(h + 1) * dv]
                s = s_ref[b, h]
                scores = lax.dot_general(qc, kc, (((1,), (1,)), ((), ())),
                                         preferred_element_type=F32) * dmask
                o = _mm(scores.astype(BF16), vc) + _mm((qc * q_dec).astype(BF16), s.astype(BF16))
                s_ref[b, h] = s * chunk_dec + lax.dot_general(
                    (kc * k_dec).astype(BF16), vc, (((0,), (0,)), ((), ())),
                    preferred_element_type=F32)
                mu = jnp.mean(o, axis=-1, keepdims=True)
                d = o - mu
                var = jnp.mean(d * d, axis=-1, keepdims=True)
                on = d * lax.rsqrt(var + GN_EPS) * gn
                gate = g_ref[b, rows, h * dv:(h + 1) * dv].astype(F32)
                gated_ref[b, rows, h * dv:(h + 1) * dv] = (
                    gate * jax.nn.sigmoid(gate) * on).astype(BF16)
        if fuse_out:
            part = _mm(gated_ref[0, :, h * dv:(h + 1) * dv], wout_ref[h * dv:(h + 1) * dv, :])
            acc = part if acc is None else acc + part
    if fuse_out:
        o_ref[0] = _layer_norm(ALPHA * x_ref[0] + acc, lng_ref[...], lnb_ref[...])


def _decay_tables(chunk):
    lg = jnp.log1p(-jnp.exp2(-5.0 - jnp.arange(RET_HEADS, dtype=F32)))
    idx = jnp.arange(chunk, dtype=F32)
    rel = idx[:, None] - idx[None, :]
    dmask = jnp.where(rel >= 0, jnp.exp(lg[:, None, None] * jnp.maximum(rel, 0.0)), 0.0)
    q_dec = jnp.exp(lg[:, None] * (idx + 1.0))[:, :, None]
    k_dec = jnp.exp(lg[:, None] * (chunk - 1.0 - idx))[:, :, None]
    chunk_dec = jnp.exp(lg * chunk)[:, None, None]
    return dmask, q_dec, k_dec, chunk_dec


def _ret_core(q, k, v, g, gn_g, j, s0, out_proj, *, bb, tt, chunk):
    b, t, qk = q.shape
    vd = v.shape[2]
    heads = RET_HEADS
    dk, dv = qk // heads, vd // heads
    dmask, q_dec, k_dec, chunk_dec = _decay_tables(chunk)
    blk = lambda n: pl.BlockSpec((bb, tt, n), lambda i, jj: (i, jj, 0))
    st_out = pl.BlockSpec((bb, heads, dk, dv), lambda i, jj: (i, 0, 0, 0))
    in_specs = [blk(qk), blk(qk), blk(vd), blk(vd), _resident_slice(gn_g.shape, j),
                _resident(dmask.shape), _resident(q_dec.shape), _resident(k_dec.shape),
                _resident(chunk_dec.shape)]
    args = [q, k, v, g, gn_g, dmask, q_dec, k_dec, chunk_dec]
    if s0 is not None:
        in_specs.append(pl.BlockSpec((None, bb, heads, dk, dv), lambda i, jj: (j, i, 0, 0, 0)))
        args.append(s0)
    n_state = 2 if s0 is None else 4
    vmem = (4 * (_nbytes((bb, tt, qk), BF16) + _nbytes((bb, tt, vd), BF16))
            + 2 * _nbytes((bb, tt, vd), BF16)
            + n_state * _nbytes((bb, heads, dk, dv), F32)
            + 16 * _nbytes((max(chunk, SUBLANES), dv), F32) + 8 * _nbytes((dk, dv), F32))
    if out_proj is None:
        out_spec, out_shape, scratch = blk(vd), jax.ShapeDtypeStruct((b, t, vd), BF16), []
    else:
        assert bb == 1
        x, w_out, (ln_g, ln_b), ln_idx = out_proj
        d = x.shape[-1]
        in_specs += [blk(d), _resident_slice(w_out.shape, j),
                     _resident_slice(ln_g.shape, ln_idx), _resident_slice(ln_b.shape, ln_idx)]
        args += [x, w_out, ln_g, ln_b]
        out_spec, out_shape = blk(d), jax.ShapeDtypeStruct((b, t, d), F32)
        scratch = [pltpu.VMEM((bb, tt, vd), BF16)]
        vmem += 8 * _nbytes((bb, tt, d), F32) + _nbytes(w_out.shape[1:], BF16)
    return pl.pallas_call(
        functools.partial(_ret_core_kernel, heads=heads, chunk=chunk, zero_init=s0 is None,
                          fuse_out=out_proj is not None),
        grid=(b // bb, t // tt),
        in_specs=in_specs,
        out_specs=[out_spec, st_out],
        out_shape=[out_shape, jax.ShapeDtypeStruct((b, heads, dk, dv), F32)],
        scratch_shapes=scratch,
        compiler_params=_params(("parallel", "arbitrary"), vmem),
        name="ret_core",
    )(*args)


def _gelu_tanh(x):
    c = math.sqrt(2.0 / math.pi)
    th = jnp.tanh(x * ((x * x) * (c * 0.044715) + c))
    hx = 0.5 * x
    return hx * th + hx


def _rglru_kernel(*refs, nb, pipelined, gate_windows):
    if pipelined:
        x_ref, xn_ref = refs[:2]
        refs = refs[2:]
    else:
        x_ref, xn_ref = refs[0], None
        refs = refs[1:]
    (win_ref, cw_ref, cb_ref, wbd_ref, ba_ref, bi_ref, lam_ref, conv0_ref, h0_ref, wout_ref,
     g_ref, b_ref, o_ref, convn_ref, hlast_ref,
     gate0_ref, gate1_ref, cin0_ref, cin1_ref, xpad_ref, a_ref, u_ref, h_ref) = refs
    gate_ref = (gate0_ref, gate1_ref)
    cin_ref = (cin0_ref, cin1_ref)
    batch_major = len(x_ref.shape) == 3
    dr = wout_ref.shape[0]
    n_slab = dr // LANES
    hist = (CONV_W - 1) * nb
    tm = a_ref.shape[1]
    nt = tm // nb
    lanes = lambda s: slice(s * LANES, (s + 1) * LANES)
    step = pl.program_id(0)

    def in_proj(src_ref, slot):
        xb = src_ref[...].reshape(tm, src_ref.shape[-1]).astype(BF16)
        for lo in range(0, dr, PROJ_CHUNK):
            cin_ref[slot][:, lo:lo + PROJ_CHUNK] = _mm(xb, win_ref[:, dr + lo:dr + lo + PROJ_CHUNK])
            yield
        for lo in range(0, dr, PROJ_CHUNK):
            gate_ref[slot][:, lo:lo + PROJ_CHUNK] = _gelu_tanh(_mm(xb, win_ref[:, lo:lo + PROJ_CHUNK]))
            yield

    @pl.when(step == 0)
    def _():
        xpad_ref[:, 0:hist, :] = conv0_ref[...]
        h_ref[...] = h0_ref[...]
        _interleave(in_proj(x_ref, 0))

    def recurrent(slot):
        for s in range(n_slab):
            if batch_major:
                for b in range(nb):
                    xpad_ref[s, pl.ds(hist + b, nt, stride=nb), :] = (
                        cin_ref[slot][b * nt:(b + 1) * nt, lanes(s)])
            else:
                xpad_ref[s, hist:hist + tm, :] = cin_ref[slot][:, lanes(s)]

        conv = None
        for j in range(CONV_W):
            term = cw_ref[j] * xpad_ref[:, j * nb:j * nb + tm, :]
            conv = term if conv is None else conv + term
        conv = cb_ref[...] + conv
        xc = jnp.concatenate([conv[s] for s in range(n_slab)], axis=1)

        tail = xpad_ref[:, tm:tm + hist, :]
        convn_ref[...] = tail
        xpad_ref[:, 0:hist, :] = tail
        yield

        lam = lam_ref[...]
        softplus_neg_lam = jnp.maximum(-lam, 0.0) + jnp.log1p(jnp.exp(-jnp.abs(lam)))
        c2 = (-0.5 * LRU_C * math.log2(math.e)) * softplus_neg_lam
        for jt, start in enumerate(gate_windows):
            cols = slice(jt * GATE_COLS, (jt + 1) * GATE_COLS)
            xw = xc[:, start:start + GATE_ROWS].astype(BF16)
            xh = xc[:, cols]
            th_a = jnp.tanh(_mm(xw, wbd_ref[0, jt]) + 0.5 * ba_ref[:, cols])
            th_i = jnp.tanh(_mm(xw, wbd_ref[1, jt]) + 0.5 * bi_ref[:, cols])
            a = jnp.exp2(c2[:, cols] * th_a + c2[:, cols])
            hx = 0.5 * xh
            y1 = 1.0 - a * a
            u = (y1 * lax.rsqrt(jnp.maximum(y1, RSQRT_FLOOR))) * (hx * th_i + hx)
            for s in range(GATE_COLS // LANES):
                a_ref[jt * (GATE_COLS // LANES) + s] = a[:, lanes(s)]
                u_ref[jt * (GATE_COLS // LANES) + s] = u[:, lanes(s)]
            yield

        h = h_ref[...]
        for t in range(nt):
            rows = slice(t * nb, (t + 1) * nb)
            h = a_ref[:, rows, :] * h + u_ref[:, rows, :]
            u_ref[:, rows, :] = h
        h_ref[...] = h
        hlast_ref[...] = h
        yield

        if batch_major:
            hs = jnp.concatenate(
                [jnp.concatenate([u_ref[s, pl.ds(b, nt, stride=nb), :] for b in range(nb)], axis=0)
                 for s in range(n_slab)], axis=1)
        else:
            hs = jnp.concatenate([u_ref[s] for s in range(n_slab)], axis=1)
        y = (gate_ref[slot][...] * hs).astype(BF16)
        x = x_ref[...].reshape(tm, x_ref.shape[-1])
        out = _layer_norm(ALPHA * x + _mm(y, wout_ref[...]), g_ref[...], b_ref[...])
        o_ref[...] = out.reshape(o_ref.shape)
        yield

    if pipelined:
        for slot in range(2):
            @pl.when(step % 2 == slot)
            def _(slot=slot):
                _interleave(recurrent(slot), in_proj(xn_ref, 1 - slot))
    else:
        _interleave(recurrent(0))


def _interleave(main, side=(), side_per_phase=2):
    side = iter(side)
    for _ in main:
        for _ in range(side_per_phase):
            next(side, None)
    for _ in side:
        pass


def _rglru(x, w, j, conv0, h0, ln, ln_idx, *, nb):
    w_in, cw, cb, wbd, ba, bi, lam, w_out, gate_windows = w
    ln_g, ln_b = ln
    dr, d = w_out.shape[1:]
    n_slab = dr // LANES
    hist = (CONV_W - 1) * nb
    if x.ndim == 3:
        nt = ROW_TILE // nb
        tm = ROW_TILE
        steps = x.shape[1] // nt
        block = (nb, nt, d)
        at = lambda i: (0, i, 0)
    else:
        tm = min(ROW_TILE, x.shape[0])
        steps = x.shape[0] // tm
        block = (tm, d)
        at = lambda i: (i, 0)
    assert tm % nb == 0 and tm >= hist and nb % SUBLANES == 0
    pipelined = steps > 1
    x_spec = pl.BlockSpec(block, at)
    x_specs, xs = [x_spec], [x]
    if pipelined:
        x_specs.append(pl.BlockSpec(block, lambda i: at(jnp.minimum(i + 1, steps - 1))))
        xs.append(x)
    vmem = ((2 + 2 * len(xs)) * _nbytes((tm, d), F32) + _nbytes(w_in.shape[1:], BF16)
            + _nbytes(wbd.shape[1:], BF16) + _nbytes(w_out.shape[1:], BF16)
            + 4 * _nbytes((hist + nb, dr), F32) + _nbytes((tm + hist, dr), F32)
            + 6 * _nbytes((tm, dr), F32) + _nbytes((nb, dr), F32) + 8 * _nbytes((tm, dr), F32))
    sl = _resident_slice
    return pl.pallas_call(
        functools.partial(_rglru_kernel, nb=nb, pipelined=pipelined, gate_windows=gate_windows),
        grid=(steps,),
        in_specs=x_specs + [sl(w_in.shape, j), sl(cw.shape, j), sl(cb.shape, j), sl(wbd.shape, j),
                            sl(ba.shape, j), sl(bi.shape, j), sl(lam.shape, j),
                            _resident(conv0.shape), _resident(h0.shape),
                            sl(w_out.shape, j), sl(ln_g.shape, ln_idx), sl(ln_b.shape, ln_idx)],
        out_specs=[x_spec,
                   pl.BlockSpec((n_slab, hist, LANES), lambda i: (0, 0, 0)),
                   pl.BlockSpec((n_slab, nb, LANES), lambda i: (0, 0, 0))],
        out_shape=[jax.ShapeDtypeStruct(x.shape, F32),
                   jax.ShapeDtypeStruct((n_slab, hist, LANES), F32),
                   jax.ShapeDtypeStruct((n_slab, nb, LANES), F32)],
        scratch_shapes=[pltpu.VMEM((tm, dr), F32), pltpu.VMEM((tm, dr), F32),
                        pltpu.VMEM((tm, dr), F32), pltpu.VMEM((tm, dr), F32),
                        pltpu.VMEM((n_slab, tm + hist, LANES), F32),
                        pltpu.VMEM((n_slab, tm, LANES), F32),
                        pltpu.VMEM((n_slab, tm, LANES), F32),
                        pltpu.VMEM((n_slab, nb, LANES), F32)],
        compiler_params=_params(("arbitrary",), vmem),
        name="rglru",
    )(*xs, w_in, cw, cb, wbd, ba, bi, lam, conv0, h0, w_out, ln_g, ln_b)


def _block_diag(blocks):
    n, bs, _ = blocks.shape
    eye = jnp.eye(n, dtype=blocks.dtype)
    return (eye[:, None, :, None] * blocks[:, :, None, :]).reshape(n * bs, n * bs)


def _gate_windows(dr, block):
    starts = []
    for c0 in range(0, dr, GATE_COLS):
        lo = c0 // block * block
        hi = ((c0 + GATE_COLS - 1) // block + 1) * block
        start = min(lo // LANES * LANES, dr - GATE_ROWS)
        assert start <= lo and hi <= start + GATE_ROWS
        starts.append(start)
    return tuple(starts)


def _prep_rec(w_in, conv_w, conv_b, w_a, b_a, w_i, b_i, lam, w_out):
    n_layers, dr, _ = w_out.shape
    n_slab = dr // LANES
    windows = _gate_windows(dr, w_a.shape[2])
    wbd = jnp.stack([
        jnp.stack([
            jnp.stack([_block_diag(w[j])[start:start + GATE_ROWS, jt * GATE_COLS:(jt + 1) * GATE_COLS]
                       for jt, start in enumerate(windows)])
            for w in (w_a, w_i)])
        for j in range(n_layers)])
    wbd = (0.5 * wbd).astype(BF16)
    row = lambda v: v.reshape(n_layers, 1, dr)
    return (w_in.astype(BF16), conv_w.reshape(n_layers, CONV_W, n_slab, 1, LANES),
            conv_b.reshape(n_layers, n_slab, 1, LANES), wbd, row(b_a), row(b_i), row(lam),
            w_out.astype(BF16), windows)


def _to_slabs(a):
    rows, dr = a.shape
    return a.reshape(rows, dr // LANES, LANES).transpose(1, 0, 2)


def _from_slabs(a):
    n_slab, rows, _ = a.shape
    return a.transpose(1, 0, 2).reshape(rows, n_slab * LANES)


def _rope_tables(pos, dk):
    half = dk // 2
    inv = ROPE_BASE ** (-jnp.arange(half, dtype=F32) / half)
    ang = pos.astype(F32)[:, None] * inv[None, :]
    return jnp.cos(ang), jnp.sin(ang)


def _to_time_major(a, b, t):
    return a.reshape(b, t, a.shape[-1]).transpose(1, 0, 2).reshape(t * b, a.shape[-1])


def _to_batch_major(a, b, t):
    return a.reshape(t, b, a.shape[-1]).transpose(1, 0, 2).reshape(b * t, a.shape[-1])


def _trunk(x, pos, state_ret, state_conv, state_lru, ln, ffn1, ffn2, ret, rec):
    b, t, d = x.shape
    m = b * t
    x = x.reshape(m, d)
    long_seq = t >= ROW_TILE
    new_ret, new_conv, new_lru = [], [], []
    for layer in range(DEPTH):
        j = layer // 2
        x = _ffn_ln(x, ffn1, layer, ln, 3 * layer)
        if layer % 2 == 0:
            w_proj, w_out, gn_g = ret
            vd, qk = w_out.shape[1:]
            cos, sin = _rope_tables(pos, qk // RET_HEADS)
            chunk = math.gcd(t, RET_CHUNK)
            bb, tt = (1, ROW_TILE) if long_seq else (SHORT_SEQ_BATCH_BLOCK, t)
            q, k, v, g = _ret_proj(x, w_proj, j, cos, sin, qk, vd)
            qkvg = (q.reshape(b, t, qk), k.reshape(b, t, qk), v.reshape(b, t, vd), g.reshape(b, t, vd))
            if long_seq:
                x, s_new = _ret_core(*qkvg, gn_g, j, state_ret,
                                     (x.reshape(b, t, d), w_out, ln, 3 * layer + 1),
                                     bb=bb, tt=tt, chunk=chunk)
                x = x.reshape(m, d)
            else:
                gated, s_new = _ret_core(*qkvg, gn_g, j, state_ret, None, bb=bb, tt=tt, chunk=chunk)
                x = _proj_ln(gated.reshape(m, vd), x, w_out, j, ln, 3 * layer + 1)
            new_ret.append(s_new)
        else:
            dr = rec[-2].shape[1]
            if state_conv is None:
                conv0 = jnp.zeros((dr // LANES, (CONV_W - 1) * b, LANES), F32)
                h0 = jnp.zeros((dr // LANES, b, LANES), F32)
            else:
                conv0 = _to_slabs(state_conv[j].transpose(1, 0, 2).reshape((CONV_W - 1) * b, dr))
                h0 = _to_slabs(state_lru[j])
            x_in = x.reshape(b, t, d) if long_seq else _to_time_major(x, b, t)
            x_out, conv_n, h_last = _rglru(x_in, rec, j, conv0, h0, ln, 3 * layer + 1, nb=b)
            x = x_out.reshape(m, d) if long_seq else _to_batch_major(x_out, b, t)
            new_conv.append(_from_slabs(conv_n).reshape(CONV_W - 1, b, dr).transpose(1, 0, 2))
            new_lru.append(_from_slabs(h_last))
        x = _ffn_ln(x, ffn2, layer, ln, 3 * layer + 2)
    return x.reshape(b, t, d), jnp.stack(new_ret), jnp.stack(new_conv), jnp.stack(new_lru)


def kernel(x_prompt, x_sample, state_ret, state_conv, state_lru, ln_g, ln_b, ffn1_w_in, ffn1_w_out,
           ffn2_w_in, ffn2_w_out, ret_w_in, ret_gn_g, ret_w_out, rec_w_in, rec_conv_w, rec_conv_b,
           rec_w_a, rec_b_a, rec_w_i, rec_b_i, rec_lam, rec_w_out):
    cast = lambda w: w.astype(BF16)
    d = ln_g.shape[-1]
    ln = (ln_g.reshape(-1, 1, d), ln_b.reshape(-1, 1, d))
    ffn1 = (cast(ffn1_w_in), cast(ffn1_w_out))
    ffn2 = (cast(ffn2_w_in), cast(ffn2_w_out))
    ret = (cast(ret_w_in), cast(ret_w_out), ret_gn_g.reshape(ret_gn_g.shape[0], 1, -1))
    rec = _prep_rec(rec_w_in, rec_conv_w, rec_conv_b, rec_w_a, rec_b_a, rec_w_i, rec_b_i, rec_lam,
                    rec_w_out)
    weights = (ln, ffn1, ffn2, ret, rec)
    tp, ts = x_prompt.shape[1], x_sample.shape[1]
    y_p, ret_p, conv_p, lru_p = _trunk(x_prompt, jnp.arange(tp, dtype=jnp.int32),
                                       None, None, None, *weights)
    y_s, ret_s, conv_s, lru_s = _trunk(x_sample, PAST_LEN + jnp.arange(ts, dtype=jnp.int32),
                                       state_ret, state_conv, state_lru, *weights)
    return (y_p, y_s, ret_p, conv_p, lru_p, ret_s, conv_s, lru_s)
```

```python
import functools
import math

import numpy as np
import jax
import jax.numpy as jnp
from jax import lax
from jax.experimental import pallas as pl
from jax.experimental.pallas import tpu as pltpu

F32 = jnp.float32
BF16 = jnp.bfloat16

DEPTH = 2
PAST_LEN = 16384
RET_HEADS = 4
RET_CHUNK = 128
ROPE_BASE = 10000.0
CONV_W = 4
LRU_C = 8.0
ALPHA = (2.0 * DEPTH) ** 0.25
LN_EPS = 1e-5
GN_EPS = 1e-6
RSQRT_FLOOR = 1e-30

V7X_SCOPED_VMEM_BYTES = 60000 * 1024
LANES = 128
SUBLANES = 8
ROW_TILE = 512
FFN_ROW_TILE = 1024
FF_CHUNK = 256
PROJ_CHUNK = 256
GATE_COLS = 256
GATE_ROWS = 512
SHORT_SEQ_BATCH_BLOCK = 4


def _params(semantics, vmem_bytes):
    return pltpu.CompilerParams(
        dimension_semantics=semantics,
        vmem_limit_bytes=int(min(V7X_SCOPED_VMEM_BYTES, vmem_bytes)),
    )


def _resident(shape):
    zeros = (0,) * len(shape)
    return pl.BlockSpec(shape, lambda *_: zeros, pipeline_mode=pl.Buffered(1))


def _resident_slice(shape, idx):
    index = (idx,) + (0,) * (len(shape) - 1)
    return pl.BlockSpec((None,) + tuple(shape[1:]), lambda *_: index, pipeline_mode=pl.Buffered(1))


def _nbytes(shape, dtype):
    return int(np.prod(shape)) * jnp.dtype(dtype).itemsize


def _layer_norm(y, g, b):
    mu = jnp.mean(y, axis=-1, keepdims=True)
    d = y - mu
    var = jnp.mean(d * d, axis=-1, keepdims=True)
    return d * lax.rsqrt(var + LN_EPS) * g + b


def _mm(a, b):
    return jnp.dot(a, b, preferred_element_type=F32)


def _ffn_ln_kernel(*refs, ff, first_tile):
    n = len(first_tile) - 1
    x_refs, (win_ref, wout_ref, g_ref, b_ref), o_refs = refs[:n], refs[n:n + 4], refs[n + 4:]

    def body(x_ref, o_ref):
        x = x_ref[...]
        xb = x.astype(BF16)
        acc = jnp.zeros(x.shape, F32)
        for lo in range(0, ff, FF_CHUNK):
            gate = _mm(xb, win_ref[:, lo:lo + FF_CHUNK])
            up = _mm(xb, win_ref[:, ff + lo:ff + lo + FF_CHUNK])
            h = (gate * jax.nn.sigmoid(gate) * up).astype(BF16)
            acc = acc + _mm(h, wout_ref[lo:lo + FF_CHUNK, :])
        o_ref[...] = _layer_norm(ALPHA * x + 0.5 * acc, g_ref[...], b_ref[...])

    if n == 1:
        body(x_refs[0], o_refs[0])
    else:
        step = pl.program_id(0)
        for i in range(n):
            pl.when((step >= first_tile[i]) & (step < first_tile[i + 1]))(
                functools.partial(body, x_refs[i], o_refs[i]))


def _ffn_ln(xs, w, layer, ln, ln_idx):
    w_in, w_out = w
    ln_g, ln_b = ln
    d = xs[0].shape[1]
    ff = w_out.shape[1]
    assert ff % FF_CHUNK == 0
    tms = [min(FFN_ROW_TILE, x.shape[0]) for x in xs]
    first_tile = [0]
    for x, tm in zip(xs, tms):
        first_tile.append(first_tile[-1] + x.shape[0] // tm)

    def rows(i):
        lo, n_tiles = first_tile[i], first_tile[i + 1] - first_tile[i]
        return pl.BlockSpec((tms[i], d), lambda s: (jnp.clip(s - lo, 0, n_tiles - 1), 0))

    specs = [rows(i) for i in range(len(xs))]
    vmem = (sum(4 * _nbytes((tm, d), F32) for tm in tms) + _nbytes(w_in.shape[1:], BF16)
            + _nbytes(w_out.shape[1:], BF16) + 8 * _nbytes((max(tms), d), F32))
    return pl.pallas_call(
        functools.partial(_ffn_ln_kernel, ff=ff, first_tile=tuple(first_tile)),
        grid=(first_tile[-1],),
        in_specs=specs + [_resident_slice(w_in.shape, layer), _resident_slice(w_out.shape, layer),
                          _resident_slice(ln_g.shape, ln_idx), _resident_slice(ln_b.shape, ln_idx)],
        out_specs=specs,
        out_shape=[jax.ShapeDtypeStruct(x.shape, F32) for x in xs],
        compiler_params=_params(("parallel",) if len(xs) == 1 else ("arbitrary",), vmem),
        name="ffn_ln",
    )(*xs, w_in, w_out, ln_g, ln_b)


def _proj_ln_kernel(a_ref, x_ref, w_ref, g_ref, b_ref, o_ref):
    y = ALPHA * x_ref[...] + _mm(a_ref[...], w_ref[...])
    o_ref[...] = _layer_norm(y, g_ref[...], b_ref[...])


def _proj_ln(a, x, w, j, ln, ln_idx):
    ln_g, ln_b = ln
    m, d = x.shape
    k = a.shape[1]
    tm = min(ROW_TILE, m)
    vmem = (2 * _nbytes((tm, k), BF16) + 4 * _nbytes((tm, d), F32) + _nbytes(w.shape[1:], BF16)
            + 4 * _nbytes((tm, d), F32))
    return pl.pallas_call(
        _proj_ln_kernel,
        grid=(m // tm,),
        in_specs=[pl.BlockSpec((tm, k), lambda i: (i, 0)), pl.BlockSpec((tm, d), lambda i: (i, 0)),
                  _resident_slice(w.shape, j),
                  _resident_slice(ln_g.shape, ln_idx), _resident_slice(ln_b.shape, ln_idx)],
        out_specs=pl.BlockSpec((tm, d), lambda i: (i, 0)),
        out_shape=jax.ShapeDtypeStruct((m, d), F32),
        compiler_params=_params(("parallel",), vmem),
        name="proj_ln",
    )(a, x, w, ln_g, ln_b)


def _ret_proj_kernel(x_ref, w_ref, cos_ref, sin_ref, q_ref, k_ref, v_ref, g_ref, *, heads, k_scale):
    xb = x_ref[...].astype(BF16)
    cos = cos_ref[...]
    sin = sin_ref[...]
    half = cos.shape[1]
    qk = q_ref.shape[1]
    vd = v_ref.shape[1]
    for col, dst, scale in ((0, q_ref, None), (qk, k_ref, k_scale)):
        p = _mm(xb, w_ref[:, col:col + qk])
        for h in range(heads):
            lo = 2 * half * h
            x1 = p[:, lo:lo + half]
            x2 = p[:, lo + half:lo + 2 * half]
            r1 = x1 * cos - x2 * sin
            r2 = x1 * sin + x2 * cos
            if scale is not None:
                r1 = r1 * scale
                r2 = r2 * scale
            dst[:, lo:lo + half] = r1.astype(BF16)
            dst[:, lo + half:lo + 2 * half] = r2.astype(BF16)
    v_ref[...] = _mm(xb, w_ref[:, 2 * qk:2 * qk + vd]).astype(BF16)
    g_ref[...] = _mm(xb, w_ref[:, 2 * qk + vd:]).astype(BF16)


def _ret_proj(x, w, j, cos, sin, qk, vd):
    m, d = x.shape
    tm = min(ROW_TILE, m)
    if cos.shape[0] < tm:
        reps = tm // cos.shape[0]
        cos, sin = jnp.tile(cos, (reps, 1)), jnp.tile(sin, (reps, 1))
    n_tab = cos.shape[0] // tm
    row = lambda n: pl.BlockSpec((tm, n), lambda i: (i, 0))
    tab = pl.BlockSpec((tm, cos.shape[1]), lambda i: (i % n_tab, 0))
    vmem = (2 * _nbytes((tm, d), F32) + _nbytes(w.shape[1:], BF16)
            + 4 * (_nbytes((tm, qk), BF16) + _nbytes((tm, vd), BF16))
            + 4 * _nbytes((tm, vd), F32))
    dk = qk // RET_HEADS
    return pl.pallas_call(
        functools.partial(_ret_proj_kernel, heads=RET_HEADS, k_scale=dk ** -0.5),
        grid=(m // tm,),
        in_specs=[row(d), _resident_slice(w.shape, j), tab, tab],
        out_specs=[row(qk), row(qk), row(vd), row(vd)],
        out_shape=[jax.ShapeDtypeStruct((m, qk), BF16), jax.ShapeDtypeStruct((m, qk), BF16),
                   jax.ShapeDtypeStruct((m, vd), BF16), jax.ShapeDtypeStruct((m, vd), BF16)],
        compiler_params=_params(("parallel",), vmem),
        name="ret_proj",
    )(x, w, cos, sin)


def _ret_core_kernel(*refs, heads, chunk, zero_init, fuse_out):
    refs = list(refs)
    q_ref, k_ref, v_ref, g_ref, gn_ref, dm_ref, qd_ref, kd_ref, cd_ref = refs[:9]
    del refs[:9]
    s0_ref = None if zero_init else refs.pop(0)
    if fuse_out:
        x_ref, wout_ref, lng_ref, lnb_ref, o_ref, s_ref, gated_ref = refs
    else:
        o_ref, s_ref = refs
        gated_ref = o_ref
    bb, tt, _ = q_ref.shape
    dk = q_ref.shape[2] // heads
    dv = v_ref.shape[2] // heads

    @pl.when(pl.program_id(1) == 0)
    def _():
        if zero_init:
            s_ref[...] = jnp.zeros(s_ref.shape, F32)
        else:
            s_ref[...] = s0_ref[...]

    acc = None
    for h in range(heads):
        dmask = dm_ref[h]
        q_dec = qd_ref[h]
        k_dec = kd_ref[h]
        chunk_dec = cd_ref[h]
        gn = gn_ref[:, h * dv:(h + 1) * dv]
        for b in range(bb):
            for c in range(tt // chunk):
                rows = slice(c * chunk, (c + 1) * chunk)
                qc = q_ref[b, rows, h * dk:(h + 1) * dk]
                kc = k_ref[b, rows, h * dk:(h + 1) * dk]
                vc = v_ref[b, rows, h * dv:(h + 1) * dv]
                s = s_ref[b, h]
                scores = lax.dot_general(qc, kc, (((1,), (1,)), ((), ())),
                                         preferred_element_type=F32) * dmask
                o = _mm(scores.astype(BF16), vc) + _mm((qc * q_dec).astype(BF16), s.astype(BF16))
                s_ref[b, h] = s * chunk_dec + lax.dot_general(
                    (kc * k_dec).astype(BF16), vc, (((0,), (0,)), ((), ())),
                    preferred_element_type=F32)
                mu = jnp.mean(o, axis=-1, keepdims=True)
                d = o - mu
                var = jnp.mean(d * d, axis=-1, keepdims=True)
                on = d * lax.rsqrt(var + GN_EPS) * gn
                gate = g_ref[b, rows, h * dv:(h + 1) * dv].astype(F32)
                gated_ref[b, rows, h * dv:(h + 1) * dv] = (
                    gate * jax.nn.sigmoid(gate) * on).astype(BF16)
        if fuse_out:
            part = _mm(gated_ref[0, :, h * dv:(h + 1) * dv], wout_ref[h * dv:(h + 1) * dv, :])
            acc = part if acc is None else acc + part
    if fuse_out:
        o_ref[0] = _layer_norm(ALPHA * x_ref[0] + acc, lng_ref[...], lnb_ref[...])


def _decay_tables(chunk):
    lg = jnp.log1p(-jnp.exp2(-5.0 - jnp.arange(RET_HEADS, dtype=F32)))
    idx = jnp.arange(chunk, dtype=F32)
    rel = idx[:, None] - idx[None, :]
    dmask = jnp.where(rel >= 0, jnp.exp(lg[:, None, None] * jnp.maximum(rel, 0.0)), 0.0)
    q_dec = jnp.exp(lg[:, None] * (idx + 1.0))[:, :, None]
    k_dec = jnp.exp(lg[:, None] * (chunk - 1.0 - idx))[:, :, None]
    chunk_dec = jnp.exp(lg * chunk)[:, None, None]
    return dmask, q_dec, k_dec, chunk_dec


def _ret_core(q, k, v, g, gn_g, j, s0, out_proj, *, bb, tt, chunk):
    b, t, qk = q.shape
    vd = v.shape[2]
    heads = RET_HEADS
    dk, dv = qk // heads, vd // heads
    dmask, q_dec, k_dec, chunk_dec = _decay_tables(chunk)
    blk = lambda n: pl.BlockSpec((bb, tt, n), lambda i, jj: (i, jj, 0))
    st_out = pl.BlockSpec((bb, heads, dk, dv), lambda i, jj: (i, 0, 0, 0))
    in_specs = [blk(qk), blk(qk), blk(vd), blk(vd), _resident_slice(gn_g.shape, j),
                _resident(dmask.shape), _resident(q_dec.shape), _resident(k_dec.shape),
                _resident(chunk_dec.shape)]
    args = [q, k, v, g, gn_g, dmask, q_dec, k_dec, chunk_dec]
    if s0 is not None:
        in_specs.append(pl.BlockSpec((None, bb, heads, dk, dv), lambda i, jj: (j, i, 0, 0, 0)))
        args.append(s0)
    n_state = 2 if s0 is None else 4
    vmem = (4 * (_nbytes((bb, tt, qk), BF16) + _nbytes((bb, tt, vd), BF16))
            + 2 * _nbytes((bb, tt, vd), BF16)
            + n_state * _nbytes((bb, heads, dk, dv), F32)
            + 16 * _nbytes((max(chunk, SUBLANES), dv), F32) + 8 * _nbytes((dk, dv), F32))
    if out_proj is None:
        out_spec, out_shape, scratch = blk(vd), jax.ShapeDtypeStruct((b, t, vd), BF16), []
    else:
        assert bb == 1
        x, w_out, (ln_g, ln_b), ln_idx = out_proj
        d = x.shape[-1]
        in_specs += [blk(d), _resident_slice(w_out.shape, j),
                     _resident_slice(ln_g.shape, ln_idx), _resident_slice(ln_b.shape, ln_idx)]
        args += [x, w_out, ln_g, ln_b]
        out_spec, out_shape = blk(d), jax.ShapeDtypeStruct((b, t, d), F32)
        scratch = [pltpu.VMEM((bb, tt, vd), BF16)]
        vmem += 8 * _nbytes((bb, tt, d), F32) + _nbytes(w_out.shape[1:], BF16)
    return pl.pallas_call(
        functools.partial(_ret_core_kernel, heads=heads, chunk=chunk, zero_init=s0 is None,
                          fuse_out=out_proj is not None),
        grid=(b // bb, t // tt),
        in_specs=in_specs,
        out_specs=[out_spec, st_out],
        out_shape=[out_shape, jax.ShapeDtypeStruct((b, heads, dk, dv), F32)],
        scratch_shapes=scratch,
        compiler_params=_params(("parallel", "arbitrary"), vmem),
        name="ret_core",
    )(*args)


def _gelu_tanh(x):
    c = math.sqrt(2.0 / math.pi)
    th = jnp.tanh(x * ((x * x) * (c * 0.044715) + c))
    hx = 0.5 * x
    return hx * th + hx


def _rglru_kernel(*refs, nb, pipelined, gate_windows):
    if pipelined:
        x_ref, xn_ref = refs[:2]
        refs = refs[2:]
    else:
        x_ref, xn_ref = refs[0], None
        refs = refs[1:]
    (win_ref, cw_ref, cb_ref, wbd_ref, ba_ref, bi_ref, lam_ref, conv0_ref, h0_ref, wout_ref,
     g_ref, b_ref, o_ref, convn_ref, hlast_ref,
     gate0_ref, gate1_ref, cin0_ref, cin1_ref, xpad_ref, a_ref, u_ref, h_ref) = refs
    gate_ref = (gate0_ref, gate1_ref)
    cin_ref = (cin0_ref, cin1_ref)
    batch_major = len(x_ref.shape) == 3
    dr = wout_ref.shape[0]
    n_slab = dr // LANES
    hist = (CONV_W - 1) * nb
    tm = a_ref.shape[1]
    nt = tm // nb
    lanes = lambda s: slice(s * LANES, (s + 1) * LANES)
    step = pl.program_id(0)

    def in_proj(src_ref, slot):
        xb = src_ref[...].reshape(tm, src_ref.shape[-1]).astype(BF16)
        for lo in range(0, dr, PROJ_CHUNK):
            cin_ref[slot][:, lo:lo + PROJ_CHUNK] = _mm(xb, win_ref[:, dr + lo:dr + lo + PROJ_CHUNK])
            yield
        for lo in range(0, dr, PROJ_CHUNK):
            gate_ref[slot][:, lo:lo + PROJ_CHUNK] = _gelu_tanh(_mm(xb, win_ref[:, lo:lo + PROJ_CHUNK]))
            yield

    @pl.when(step == 0)
    def _():
        xpad_ref[:, 0:hist, :] = conv0_ref[...]
        h_ref[...] = h0_ref[...]
        _interleave(in_proj(x_ref, 0))

    def recurrent(slot):
        for s in range(n_slab):
            if batch_major:
                for b in range(nb):
                    xpad_ref[s, pl.ds(hist + b, nt, stride=nb), :] = (
                        cin_ref[slot][b * nt:(b + 1) * nt, lanes(s)])
            else:
                xpad_ref[s, hist:hist + tm, :] = cin_ref[slot][:, lanes(s)]

        conv = None
        for j in range(CONV_W):
            term = cw_ref[j] * xpad_ref[:, j * nb:j * nb + tm, :]
            conv = term if conv is None else conv + term
        conv = cb_ref[...] + conv
        xc = jnp.concatenate([conv[s] for s in range(n_slab)], axis=1)

        tail = xpad_ref[:, tm:tm + hist, :]
        convn_ref[...] = tail
        xpad_ref[:, 0:hist, :] = tail
        yield

        lam = lam_ref[...]
        softplus_neg_lam = jnp.maximum(-lam, 0.0) + jnp.log1p(jnp.exp(-jnp.abs(lam)))
        c2 = (-0.5 * LRU_C * math.log2(math.e)) * softplus_neg_lam
        for jt, start in enumerate(gate_windows):
            cols = slice(jt * GATE_COLS, (jt + 1) * GATE_COLS)
            xw = xc[:, start:start + GATE_ROWS].astype(BF16)
            xh = xc[:, cols]
            th_a = jnp.tanh(_mm(xw, wbd_ref[0, jt]) + 0.5 * ba_ref[:, cols])
            th_i = jnp.tanh(_mm(xw, wbd_ref[1, jt]) + 0.5 * bi_ref[:, cols])
            a = jnp.exp2(c2[:, cols] * th_a + c2[:, cols])
            hx = 0.5 * xh
            y1 = 1.0 - a * a
            u = (y1 * lax.rsqrt(jnp.maximum(y1, RSQRT_FLOOR))) * (hx * th_i + hx)
            for s in range(GATE_COLS // LANES):
                a_ref[jt * (GATE_COLS // LANES) + s] = a[:, lanes(s)]
                u_ref[jt * (GATE_COLS // LANES) + s] = u[:, lanes(s)]
            yield

        h = h_ref[...]
        for t in range(nt):
            rows = slice(t * nb, (t + 1) * nb)
            h = a_ref[:, rows, :] * h + u_ref[:, rows, :]
            u_ref[:, rows, :] = h
        h_ref[...] = h
        hlast_ref[...] = h
        yield

        if batch_major:
            hs = jnp.concatenate(
                [jnp.concatenate([u_ref[s, pl.ds(b, nt, stride=nb), :] for b in range(nb)], axis=0)
                 for s in range(n_slab)], axis=1)
        else:
            hs = jnp.concatenate([u_ref[s] for s in range(n_slab)], axis=1)
        y = (gate_ref[slot][...] * hs).astype(BF16)
        x = x_ref[...].reshape(tm, x_ref.shape[-1])
        out = _layer_norm(ALPHA * x + _mm(y, wout_ref[...]), g_ref[...], b_ref[...])
        o_ref[...] = out.reshape(o_ref.shape)
        yield

    if pipelined:
        for slot in range(2):
            @pl.when(step % 2 == slot)
            def _(slot=slot):
                _interleave(recurrent(slot), in_proj(xn_ref, 1 - slot))
    else:
        _interleave(recurrent(0))


def _interleave(main, side=(), side_per_phase=2):
    side = iter(side)
    for _ in main:
        for _ in range(side_per_phase):
            next(side, None)
    for _ in side:
        pass


def _rglru(x, w, j, conv0, h0, ln, ln_idx, *, nb):
    w_in, cw, cb, wbd, ba, bi, lam, w_out, gate_windows = w
    ln_g, ln_b = ln
    dr, d = w_out.shape[1:]
    n_slab = dr // LANES
    hist = (CONV_W - 1) * nb
    if x.ndim == 3:
        nt = ROW_TILE // nb
        tm = ROW_TILE
        steps = x.shape[1] // nt
        block = (nb, nt, d)
        at = lambda i: (0, i, 0)
    else:
        tm = min(ROW_TILE, x.shape[0])
        steps = x.shape[0] // tm
        block = (tm, d)
        at = lambda i: (i, 0)
    assert tm % nb == 0 and tm >= hist and nb % SUBLANES == 0
    pipelined = steps > 1
    x_spec = pl.BlockSpec(block, at)
    x_specs, xs = [x_spec], [x]
    if pipelined:
        x_specs.append(pl.BlockSpec(block, lambda i: at(jnp.minimum(i + 1, steps - 1))))
        xs.append(x)
    vmem = ((2 + 2 * len(xs)) * _nbytes((tm, d), F32) + _nbytes(w_in.shape[1:], BF16)
            + _nbytes(wbd.shape[1:], BF16) + _nbytes(w_out.shape[1:], BF16)
            + 4 * _nbytes((hist + nb, dr), F32) + _nbytes((tm + hist, dr), F32)
            + 6 * _nbytes((tm, dr), F32) + _nbytes((nb, dr), F32) + 8 * _nbytes((tm, dr), F32))
    sl = _resident_slice
    return pl.pallas_call(
        functools.partial(_rglru_kernel, nb=nb, pipelined=pipelined, gate_windows=gate_windows),
        grid=(steps,),
        in_specs=x_specs + [sl(w_in.shape, j), sl(cw.shape, j), sl(cb.shape, j), sl(wbd.shape, j),
                            sl(ba.shape, j), sl(bi.shape, j), sl(lam.shape, j),
                            _resident(conv0.shape), _resident(h0.shape),
                            sl(w_out.shape, j), sl(ln_g.shape, ln_idx), sl(ln_b.shape, ln_idx)],
        out_specs=[x_spec,
                   pl.BlockSpec((n_slab, hist, LANES), lambda i: (0, 0, 0)),
                   pl.BlockSpec((n_slab, nb, LANES), lambda i: (0, 0, 0))],
        out_shape=[jax.ShapeDtypeStruct(x.shape, F32),
                   jax.ShapeDtypeStruct((n_slab, hist, LANES), F32),
                   jax.ShapeDtypeStruct((n_slab, nb, LANES), F32)],
        scratch_shapes=[pltpu.VMEM((tm, dr), F32), pltpu.VMEM((tm, dr), F32),
                        pltpu.VMEM((tm, dr), F32), pltpu.VMEM((tm, dr), F32),
                        pltpu.VMEM((n_slab, tm + hist, LANES), F32),
                        pltpu.VMEM((n_slab, tm, LANES), F32),
                        pltpu.VMEM((n_slab, tm, LANES), F32),
                        pltpu.VMEM((n_slab, nb, LANES), F32)],
        compiler_params=_params(("arbitrary",), vmem),
        name="rglru",
    )(*xs, w_in, cw, cb, wbd, ba, bi, lam, conv0, h0, w_out, ln_g, ln_b)


def _block_diag(blocks):
    n, bs, _ = blocks.shape
    cols = [jnp.pad(blocks[i], ((i * bs, (n - 1 - i) * bs), (0, 0))) for i in range(n)]
    return jnp.concatenate(cols, axis=1)


def _gate_windows(dr, block):
    starts = []
    for c0 in range(0, dr, GATE_COLS):
        lo = c0 // block * block
        hi = ((c0 + GATE_COLS - 1) // block + 1) * block
        start = min(lo // LANES * LANES, dr - GATE_ROWS)
        assert start <= lo and hi <= start + GATE_ROWS
        starts.append(start)
    return tuple(starts)


def _prep_rec(w_in, conv_w, conv_b, w_a, b_a, w_i, b_i, lam, w_out):
    n_layers, dr, _ = w_out.shape
    n_slab = dr // LANES
    windows = _gate_windows(dr, w_a.shape[2])
    wbd = jnp.stack([
        jnp.stack([
            jnp.stack([_block_diag(w[j])[start:start + GATE_ROWS, jt * GATE_COLS:(jt + 1) * GATE_COLS]
                       for jt, start in enumerate(windows)])
            for w in (w_a, w_i)])
        for j in range(n_layers)])
    wbd = (0.5 * wbd).astype(BF16)
    row = lambda v: v.reshape(n_layers, 1, dr)
    return (w_in.astype(BF16), conv_w.reshape(n_layers, CONV_W, n_slab, 1, LANES),
            conv_b.reshape(n_layers, n_slab, 1, LANES), wbd, row(b_a), row(b_i), row(lam),
            w_out.astype(BF16), windows)


def _to_slabs(a):
    rows, dr = a.shape
    return a.reshape(rows, dr // LANES, LANES).transpose(1, 0, 2)


def _from_slabs(a):
    n_slab, rows, _ = a.shape
    return a.transpose(1, 0, 2).reshape(rows, n_slab * LANES)


def _rope_tables(pos, dk):
    half = dk // 2
    inv = ROPE_BASE ** (-jnp.arange(half, dtype=F32) / half)
    ang = pos.astype(F32)[:, None] * inv[None, :]
    return jnp.cos(ang), jnp.sin(ang)


def _to_time_major(a, b, t):
    return a.reshape(b, t, a.shape[-1]).transpose(1, 0, 2).reshape(t * b, a.shape[-1])


def _to_batch_major(a, b, t):
    return a.reshape(t, b, a.shape[-1]).transpose(1, 0, 2).reshape(b * t, a.shape[-1])


def _trunk(x, pos, state_ret, state_conv, state_lru, ln, ffn1, ffn2, ret, rec):
    b, t, d = x.shape
    m = b * t
    x = x.reshape(m, d)
    long_seq = t >= ROW_TILE
    new_ret, new_conv, new_lru = [], [], []
    for layer in range(DEPTH):
        j = layer // 2
        x = yield (x, ffn1, layer, 3 * layer)
        if layer % 2 == 0:
            w_proj, w_out, gn_g = ret
            vd, qk = w_out.shape[1:]
            cos, sin = _rope_tables(pos, qk // RET_HEADS)
            chunk = math.gcd(t, RET_CHUNK)
            bb, tt = (1, ROW_TILE) if long_seq else (SHORT_SEQ_BATCH_BLOCK, t)
            q, k, v, g = _ret_proj(x, w_proj, j, cos, sin, qk, vd)
            qkvg = (q.reshape(b, t, qk), k.reshape(b, t, qk), v.reshape(b, t, vd), g.reshape(b, t, vd))
            if long_seq:
                x, s_new = _ret_core(*qkvg, gn_g, j, state_ret,
                                     (x.reshape(b, t, d), w_out, ln, 3 * layer + 1),
                                     bb=bb, tt=tt, chunk=chunk)
                x = x.reshape(m, d)
            else:
                gated, s_new = _ret_core(*qkvg, gn_g, j, state_ret, None, bb=bb, tt=tt, chunk=chunk)
                x = _proj_ln(gated.reshape(m, vd), x, w_out, j, ln, 3 * layer + 1)
            new_ret.append(s_new)
        else:
            dr = rec[-2].shape[1]
            if state_conv is None:
                conv0 = jnp.zeros((dr // LANES, (CONV_W - 1) * b, LANES), F32)
                h0 = jnp.zeros((dr // LANES, b, LANES), F32)
            else:
                conv0 = _to_slabs(state_conv[j].transpose(1, 0, 2).reshape((CONV_W - 1) * b, dr))
                h0 = _to_slabs(state_lru[j])
            x_in = x.reshape(b, t, d) if long_seq else _to_time_major(x, b, t)
            x_out, conv_n, h_last = _rglru(x_in, rec, j, conv0, h0, ln, 3 * layer + 1, nb=b)
            x = x_out.reshape(m, d) if long_seq else _to_batch_major(x_out, b, t)
            new_conv.append(_from_slabs(conv_n).reshape(CONV_W - 1, b, dr).transpose(1, 0, 2))
            new_lru.append(_from_slabs(h_last))
        x = yield (x, ffn2, layer, 3 * layer + 2)
    return x.reshape(b, t, d), jnp.stack(new_ret), jnp.stack(new_conv), jnp.stack(new_lru)


def _run_trunks(trunks, ln):
    requests = [next(g) for g in trunks]
    results = [None] * len(trunks)
    while results[0] is None:
        _, w, layer, ln_idx = requests[0]
        outs = _ffn_ln([r[0] for r in requests], w, layer, ln, ln_idx)
        for i, g in enumerate(trunks):
            try:
                requests[i] = g.send(outs[i])
            except StopIteration as done:
                results[i] = done.value
    return results


def kernel(x_prompt, x_sample, state_ret, state_conv, state_lru, ln_g, ln_b, ffn1_w_in, ffn1_w_out,
           ffn2_w_in, ffn2_w_out, ret_w_in, ret_gn_g, ret_w_out, rec_w_in, rec_conv_w, rec_conv_b,
           rec_w_a, rec_b_a, rec_w_i, rec_b_i, rec_lam, rec_w_out):
    cast = lambda w: w.astype(BF16)
    d = ln_g.shape[-1]
    ln = (ln_g.reshape(-1, 1, d), ln_b.reshape(-1, 1, d))
    ffn1 = (cast(ffn1_w_in), cast(ffn1_w_out))
    ffn2 = (cast(ffn2_w_in), cast(ffn2_w_out))
    ret = (cast(ret_w_in), cast(ret_w_out), ret_gn_g.reshape(ret_gn_g.shape[0], 1, -1))
    rec = _prep_rec(rec_w_in, rec_conv_w, rec_conv_b, rec_w_a, rec_b_a, rec_w_i, rec_b_i, rec_lam,
                    rec_w_out)
    weights = (ln, ffn1, ffn2, ret, rec)
    tp, ts = x_prompt.shape[1], x_sample.shape[1]
    prompt = _trunk(x_prompt, jnp.arange(tp, dtype=jnp.int32), None, None, None, *weights)
    sample = _trunk(x_sample, PAST_LEN + jnp.arange(ts, dtype=jnp.int32),
                    state_ret, state_conv, state_lru, *weights)
    (y_p, ret_p, conv_p, lru_p), (y_s, ret_s, conv_s, lru_s) = _run_trunks([prompt, sample], ln)
    return (y_p, y_s, ret_p, conv_p, lru_p, ret_s, conv_s, lru_s)
```

```python
import functools
import math

import numpy as np
import jax
import jax.numpy as jnp
from jax import lax
from jax.experimental import pallas as pl
from jax.experimental.pallas import tpu as pltpu

F32 = jnp.float32
BF16 = jnp.bfloat16

DEPTH = 2
PAST_LEN = 16384
RET_HEADS = 4
RET_CHUNK = 128
RET_LONG_CHUNK = 256
ROPE_BASE = 10000.0
CONV_W = 4
LRU_C = 8.0
ALPHA = (2.0 * DEPTH) ** 0.25
LN_EPS = 1e-5
GN_EPS = 1e-6
RSQRT_FLOOR = 1e-30

V7X_SCOPED_VMEM_BYTES = 60000 * 1024
LANES = 128
SUBLANES = 8
ROW_TILE = 512
FFN_ROW_TILE = 1024
FF_CHUNK = 256
PROJ_CHUNK = 256
GATE_COLS = 256
GATE_ROWS = 512
SHORT_SEQ_BATCH_BLOCK = 4


def _params(semantics, vmem_bytes):
    return pltpu.CompilerParams(
        dimension_semantics=semantics,
        vmem_limit_bytes=int(min(V7X_SCOPED_VMEM_BYTES, vmem_bytes)),
    )


def _resident(shape):
    zeros = (0,) * len(shape)
    return pl.BlockSpec(shape, lambda *_: zeros, pipeline_mode=pl.Buffered(1))


def _resident_slice(shape, idx):
    index = (idx,) + (0,) * (len(shape) - 1)
    return pl.BlockSpec((None,) + tuple(shape[1:]), lambda *_: index, pipeline_mode=pl.Buffered(1))


def _nbytes(shape, dtype):
    return int(np.prod(shape)) * jnp.dtype(dtype).itemsize


def _layer_norm(y, g, b):
    mu = jnp.mean(y, axis=-1, keepdims=True)
    d = y - mu
    var = jnp.mean(d * d, axis=-1, keepdims=True)
    return d * lax.rsqrt(var + LN_EPS) * g + b


def _mm(a, b):
    return jnp.dot(a, b, preferred_element_type=F32)


def _ffn_ln_kernel(*refs, ff, first_tile):
    n = len(first_tile) - 1
    x_refs, (win_ref, wout_ref, g_ref, b_ref), o_refs = refs[:n], refs[n:n + 4], refs[n + 4:]

    def body(x_ref, o_ref):
        x = x_ref[...]
        xb = x.astype(BF16)
        acc = jnp.zeros(x.shape, F32)
        for lo in range(0, ff, FF_CHUNK):
            gate = _mm(xb, win_ref[:, lo:lo + FF_CHUNK])
            up = _mm(xb, win_ref[:, ff + lo:ff + lo + FF_CHUNK])
            h = (gate * jax.nn.sigmoid(gate) * up).astype(BF16)
            acc = acc + _mm(h, wout_ref[lo:lo + FF_CHUNK, :])
        o_ref[...] = _layer_norm(ALPHA * x + 0.5 * acc, g_ref[...], b_ref[...])

    if n == 1:
        body(x_refs[0], o_refs[0])
    else:
        step = pl.program_id(0)
        for i in range(n):
            pl.when((step >= first_tile[i]) & (step < first_tile[i + 1]))(
                functools.partial(body, x_refs[i], o_refs[i]))


def _ffn_ln(xs, w, layer, ln, ln_idx):
    w_in, w_out = w
    ln_g, ln_b = ln
    d = xs[0].shape[1]
    ff = w_out.shape[1]
    assert ff % FF_CHUNK == 0
    tms = [min(FFN_ROW_TILE, x.shape[0]) for x in xs]
    first_tile = [0]
    for x, tm in zip(xs, tms):
        first_tile.append(first_tile[-1] + x.shape[0] // tm)

    def rows(i):
        lo, n_tiles = first_tile[i], first_tile[i + 1] - first_tile[i]
        return pl.BlockSpec((tms[i], d), lambda s: (jnp.clip(s - lo, 0, n_tiles - 1), 0))

    specs = [rows(i) for i in range(len(xs))]
    vmem = (sum(4 * _nbytes((tm, d), F32) for tm in tms) + _nbytes(w_in.shape[1:], BF16)
            + _nbytes(w_out.shape[1:], BF16) + 8 * _nbytes((max(tms), d), F32))
    return pl.pallas_call(
        functools.partial(_ffn_ln_kernel, ff=ff, first_tile=tuple(first_tile)),
        grid=(first_tile[-1],),
        in_specs=specs + [_resident_slice(w_in.shape, layer), _resident_slice(w_out.shape, layer),
                          _resident_slice(ln_g.shape, ln_idx), _resident_slice(ln_b.shape, ln_idx)],
        out_specs=specs,
        out_shape=[jax.ShapeDtypeStruct(x.shape, F32) for x in xs],
        compiler_params=_params(("parallel",) if len(xs) == 1 else ("arbitrary",), vmem),
        name="ffn_ln",
    )(*xs, w_in, w_out, ln_g, ln_b)


def _proj_ln_kernel(a_ref, x_ref, w_ref, g_ref, b_ref, o_ref):
    y = ALPHA * x_ref[...] + _mm(a_ref[...], w_ref[...])
    o_ref[...] = _layer_norm(y, g_ref[...], b_ref[...])


def _proj_ln(a, x, w, j, ln, ln_idx):
    ln_g, ln_b = ln
    m, d = x.shape
    k = a.shape[1]
    tm = min(ROW_TILE, m)
    vmem = (2 * _nbytes((tm, k), BF16) + 4 * _nbytes((tm, d), F32) + _nbytes(w.shape[1:], BF16)
            + 4 * _nbytes((tm, d), F32))
    return pl.pallas_call(
        _proj_ln_kernel,
        grid=(m // tm,),
        in_specs=[pl.BlockSpec((tm, k), lambda i: (i, 0)), pl.BlockSpec((tm, d), lambda i: (i, 0)),
                  _resident_slice(w.shape, j),
                  _resident_slice(ln_g.shape, ln_idx), _resident_slice(ln_b.shape, ln_idx)],
        out_specs=pl.BlockSpec((tm, d), lambda i: (i, 0)),
        out_shape=jax.ShapeDtypeStruct((m, d), F32),
        compiler_params=_params(("parallel",), vmem),
        name="proj_ln",
    )(a, x, w, ln_g, ln_b)


def _ret_proj_kernel(x_ref, w_ref, cos_ref, sin_ref, q_ref, k_ref, v_ref, g_ref, *, heads, k_scale):
    xb = x_ref[...].astype(BF16)
    cos = cos_ref[...]
    sin = sin_ref[...]
    half = cos.shape[1]
    qk = q_ref.shape[1]
    vd = v_ref.shape[1]
    for col, dst, scale in ((0, q_ref, None), (qk, k_ref, k_scale)):
        p = _mm(xb, w_ref[:, col:col + qk])
        for h in range(heads):
            lo = 2 * half * h
            x1 = p[:, lo:lo + half]
            x2 = p[:, lo + half:lo + 2 * half]
            r1 = x1 * cos - x2 * sin
            r2 = x1 * sin + x2 * cos
            if scale is not None:
                r1 = r1 * scale
                r2 = r2 * scale
            dst[:, lo:lo + half] = r1.astype(BF16)
            dst[:, lo + half:lo + 2 * half] = r2.astype(BF16)
    v_ref[...] = _mm(xb, w_ref[:, 2 * qk:2 * qk + vd]).astype(BF16)
    g_ref[...] = _mm(xb, w_ref[:, 2 * qk + vd:]).astype(BF16)


def _ret_proj(x, w, j, cos, sin, qk, vd):
    m, d = x.shape
    tm = min(ROW_TILE, m)
    if cos.shape[0] < tm:
        reps = tm // cos.shape[0]
        cos, sin = jnp.tile(cos, (reps, 1)), jnp.tile(sin, (reps, 1))
    n_tab = cos.shape[0] // tm
    row = lambda n: pl.BlockSpec((tm, n), lambda i: (i, 0))
    tab = pl.BlockSpec((tm, cos.shape[1]), lambda i: (i % n_tab, 0))
    vmem = (2 * _nbytes((tm, d), F32) + _nbytes(w.shape[1:], BF16)
            + 4 * (_nbytes((tm, qk), BF16) + _nbytes((tm, vd), BF16))
            + 4 * _nbytes((tm, vd), F32))
    dk = qk // RET_HEADS
    return pl.pallas_call(
        functools.partial(_ret_proj_kernel, heads=RET_HEADS, k_scale=dk ** -0.5),
        grid=(m // tm,),
        in_specs=[row(d), _resident_slice(w.shape, j), tab, tab],
        out_specs=[row(qk), row(qk), row(vd), row(vd)],
        out_shape=[jax.ShapeDtypeStruct((m, qk), BF16), jax.ShapeDtypeStruct((m, qk), BF16),
                   jax.ShapeDtypeStruct((m, vd), BF16), jax.ShapeDtypeStruct((m, vd), BF16)],
        compiler_params=_params(("parallel",), vmem),
        name="ret_proj",
    )(x, w, cos, sin)


def _retention_chunk(qc, kc, vc, s, dmask, q_dec, k_dec, chunk_dec):
    scores = lax.dot_general(qc, kc, (((1,), (1,)), ((), ())), preferred_element_type=F32) * dmask
    o = _mm(scores.astype(BF16), vc) + _mm((qc * q_dec).astype(BF16), s.astype(BF16))
    s_new = s * chunk_dec + lax.dot_general((kc * k_dec).astype(BF16), vc, (((0,), (0,)), ((), ())),
                                            preferred_element_type=F32)
    return o, s_new


def _norm_gate(o, gate, gn):
    mu = jnp.mean(o, axis=-1, keepdims=True)
    d = o - mu
    var = jnp.mean(d * d, axis=-1, keepdims=True)
    on = d * lax.rsqrt(var + GN_EPS) * gn
    gate = gate.astype(F32)
    return (gate * jax.nn.sigmoid(gate) * on).astype(BF16)


def _ret_core_kernel(q_ref, k_ref, v_ref, g_ref, gn_ref, dm_ref, qd_ref, kd_ref, cd_ref, s0_ref,
                     o_ref, s_ref, *, heads, chunk):
    bb, tt, _ = q_ref.shape
    dk = q_ref.shape[2] // heads
    dv = v_ref.shape[2] // heads

    @pl.when(pl.program_id(1) == 0)
    def _():
        s_ref[...] = s0_ref[...]

    for b in range(bb):
        for c in range(tt // chunk):
            for h in range(heads):
                rows = slice(c * chunk, (c + 1) * chunk)
                qk_cols = slice(h * dk, (h + 1) * dk)
                v_cols = slice(h * dv, (h + 1) * dv)
                o, s_ref[b, h] = _retention_chunk(
                    q_ref[b, rows, qk_cols], k_ref[b, rows, qk_cols], v_ref[b, rows, v_cols],
                    s_ref[b, h], dm_ref[h], qd_ref[h], kd_ref[h], cd_ref[h])
                o_ref[b, rows, v_cols] = _norm_gate(o, g_ref[b, rows, v_cols], gn_ref[:, v_cols])


def _ret_fused_kernel(q_ref, k_ref, v_ref, g_ref, x_ref, gn_ref, dm_ref, qd_ref, kd_ref, cd_ref,
                      wout_ref, lng_ref, lnb_ref, o_ref, s_ref, ret0_ref, ret1_ref, gated_ref,
                      *, heads, chunk, tiles_per_row, n_tiles):
    ret_ref = (ret0_ref, ret1_ref)
    tt = q_ref.shape[1]
    dk = q_ref.shape[2] // heads
    dv = v_ref.shape[2] // heads
    d_model = o_ref.shape[2]
    step = pl.program_id(0)
    chunks = [slice(c * chunk, (c + 1) * chunk) for c in range(tt // chunk)]
    v_cols = lambda h: slice(h * dv, (h + 1) * dv)

    @pl.when((step % tiles_per_row == 0) & (step < n_tiles))
    def _():
        s_ref[...] = jnp.zeros(s_ref.shape, F32)

    def attend(slot):
        for rows in chunks:
            for h in range(heads):
                qk_cols = slice(h * dk, (h + 1) * dk)
                ret_ref[slot][rows, v_cols(h)], s_ref[0, h] = _retention_chunk(
                    q_ref[0, rows, qk_cols], k_ref[0, rows, qk_cols], v_ref[0, rows, v_cols(h)],
                    s_ref[0, h], dm_ref[h], qd_ref[h], kd_ref[h], cd_ref[h])
                yield

    def finish(slot):
        for h in range(heads):
            for rows in chunks:
                gated_ref[rows, v_cols(h)] = _norm_gate(
                    ret_ref[slot][rows, v_cols(h)], g_ref[0, rows, v_cols(h)], gn_ref[:, v_cols(h)])
                yield
            for lo in range(0, d_model, PROJ_CHUNK):
                cols = slice(lo, lo + PROJ_CHUNK)
                part = _mm(gated_ref[:, v_cols(h)], wout_ref[v_cols(h), cols])
                o_ref[0, :, cols] = part if h == 0 else o_ref[0, :, cols] + part
                yield
        o_ref[0] = _layer_norm(ALPHA * x_ref[0] + o_ref[0], lng_ref[...], lnb_ref[...])
        yield

    @pl.when(step == 0)
    def _():
        _interleave(attend(0))

    for slot in range(2):
        @pl.when((step > 0) & (step < n_tiles) & (step % 2 == slot))
        def _(slot=slot):
            _interleave(attend(slot), finish(1 - slot))

    @pl.when(step == n_tiles)
    def _():
        _interleave(finish((n_tiles - 1) % 2))


def _decay_tables(chunk):
    lg = jnp.log1p(-jnp.exp2(-5.0 - jnp.arange(RET_HEADS, dtype=F32)))
    idx = jnp.arange(chunk, dtype=F32)
    rel = idx[:, None] - idx[None, :]
    dmask = jnp.where(rel >= 0, jnp.exp(lg[:, None, None] * jnp.maximum(rel, 0.0)), 0.0)
    q_dec = jnp.exp(lg[:, None] * (idx + 1.0))[:, :, None]
    k_dec = jnp.exp(lg[:, None] * (chunk - 1.0 - idx))[:, :, None]
    chunk_dec = jnp.exp(lg * chunk)[:, None, None]
    return dmask, q_dec, k_dec, chunk_dec


def _ret_core(q, k, v, g, gn_g, j, s0, *, bb, tt, chunk):
    b, t, qk = q.shape
    vd = v.shape[2]
    heads = RET_HEADS
    dk, dv = qk // heads, vd // heads
    tables = _decay_tables(chunk)
    blk = lambda n: pl.BlockSpec((bb, tt, n), lambda i, jj: (i, jj, 0))
    vmem = (4 * (_nbytes((bb, tt, qk), BF16) + _nbytes((bb, tt, vd), BF16))
            + 2 * _nbytes((bb, tt, vd), BF16) + 4 * _nbytes((bb, heads, dk, dv), F32)
            + 16 * _nbytes((max(chunk, SUBLANES), dv), F32) + 8 * _nbytes((dk, dv), F32))
    return pl.pallas_call(
        functools.partial(_ret_core_kernel, heads=heads, chunk=chunk),
        grid=(b // bb, t // tt),
        in_specs=[blk(qk), blk(qk), blk(vd), blk(vd), _resident_slice(gn_g.shape, j)]
        + [_resident(tab.shape) for tab in tables]
        + [pl.BlockSpec((None, bb, heads, dk, dv), lambda i, jj: (j, i, 0, 0, 0))],
        out_specs=[blk(vd), pl.BlockSpec((bb, heads, dk, dv), lambda i, jj: (i, 0, 0, 0))],
        out_shape=[jax.ShapeDtypeStruct((b, t, vd), BF16),
                   jax.ShapeDtypeStruct((b, heads, dk, dv), F32)],
        compiler_params=_params(("parallel", "arbitrary"), vmem),
        name="ret_core",
    )(q, k, v, g, gn_g, *tables, s0)


def _ret_fused(q, k, v, g, x, gn_g, w_out, j, ln, ln_idx, *, tt, chunk):
    ln_g, ln_b = ln
    b, t, qk = q.shape
    vd, d = v.shape[2], x.shape[2]
    heads = RET_HEADS
    dk, dv = qk // heads, vd // heads
    tables = _decay_tables(chunk)
    tiles_per_row = t // tt
    n_tiles = b * tiles_per_row
    assert tiles_per_row >= 2

    def tile_block(n, lag):
        def index(i):
            tile = jnp.clip(i - lag, 0, n_tiles - 1)
            return (tile // tiles_per_row, tile % tiles_per_row, 0)
        return pl.BlockSpec((1, tt, n), index)

    state = pl.BlockSpec((1, heads, dk, dv),
                         lambda i: (jnp.minimum(i, n_tiles - 1) // tiles_per_row, 0, 0, 0))
    vmem = (4 * (_nbytes((tt, qk), BF16) + _nbytes((tt, vd), BF16)) + 2 * _nbytes((tt, vd), BF16)
            + 4 * _nbytes((tt, d), F32) + 2 * _nbytes((heads, dk, dv), F32)
            + _nbytes(w_out.shape[1:], BF16) + 2 * _nbytes((tt, vd), F32) + _nbytes((tt, vd), BF16)
            + 16 * _nbytes((chunk, dv), F32) + 8 * _nbytes((dk, dv), F32) + 4 * _nbytes((tt, d), F32))
    return pl.pallas_call(
        functools.partial(_ret_fused_kernel, heads=heads, chunk=chunk, tiles_per_row=tiles_per_row,
                          n_tiles=n_tiles),
        grid=(n_tiles + 1,),
        in_specs=[tile_block(qk, 0), tile_block(qk, 0), tile_block(vd, 0), tile_block(vd, 1),
                  tile_block(d, 1), _resident_slice(gn_g.shape, j)]
        + [_resident(tab.shape) for tab in tables]
        + [_resident_slice(w_out.shape, j), _resident_slice(ln_g.shape, ln_idx),
           _resident_slice(ln_b.shape, ln_idx)],
        out_specs=[tile_block(d, 1), state],
        out_shape=[jax.ShapeDtypeStruct((b, t, d), F32),
                   jax.ShapeDtypeStruct((b, heads, dk, dv), F32)],
        scratch_shapes=[pltpu.VMEM((tt, vd), F32), pltpu.VMEM((tt, vd), F32),
                        pltpu.VMEM((tt, vd), BF16)],
        compiler_params=_params(("arbitrary",), vmem),
        name="ret_fused",
    )(q, k, v, g, x, gn_g, *tables, w_out, ln_g, ln_b)


def _gelu_tanh(x):
    c = math.sqrt(2.0 / math.pi)
    th = jnp.tanh(x * ((x * x) * (c * 0.044715) + c))
    hx = 0.5 * x
    return hx * th + hx


def _rglru_kernel(*refs, nb, pipelined, gate_windows):
    if pipelined:
        x_ref, xn_ref = refs[:2]
        refs = refs[2:]
    else:
        x_ref, xn_ref = refs[0], None
        refs = refs[1:]
    (win_ref, cw_ref, cb_ref, wbd_ref, ba_ref, bi_ref, lam_ref, conv0_ref, h0_ref, wout_ref,
     g_ref, b_ref, o_ref, convn_ref, hlast_ref,
     gate0_ref, gate1_ref, cin0_ref, cin1_ref, xpad_ref, a_ref, u_ref, h_ref) = refs
    gate_ref = (gate0_ref, gate1_ref)
    cin_ref = (cin0_ref, cin1_ref)
    batch_major = len(x_ref.shape) == 3
    dr = wout_ref.shape[0]
    n_slab = dr // LANES
    hist = (CONV_W - 1) * nb
    tm = a_ref.shape[1]
    nt = tm // nb
    lanes = lambda s: slice(s * LANES, (s + 1) * LANES)
    step = pl.program_id(0)

    def in_proj(src_ref, slot):
        xb = src_ref[...].reshape(tm, src_ref.shape[-1]).astype(BF16)
        for lo in range(0, dr, PROJ_CHUNK):
            cin_ref[slot][:, lo:lo + PROJ_CHUNK] = _mm(xb, win_ref[:, dr + lo:dr + lo + PROJ_CHUNK])
            yield
        for lo in range(0, dr, PROJ_CHUNK):
            gate_ref[slot][:, lo:lo + PROJ_CHUNK] = _gelu_tanh(_mm(xb, win_ref[:, lo:lo + PROJ_CHUNK]))
            yield

    @pl.when(step == 0)
    def _():
        xpad_ref[:, 0:hist, :] = conv0_ref[...]
        h_ref[...] = h0_ref[...]
        _interleave(in_proj(x_ref, 0))

    def recurrent(slot):
        for s in range(n_slab):
            if batch_major:
                for b in range(nb):
                    xpad_ref[s, pl.ds(hist + b, nt, stride=nb), :] = (
                        cin_ref[slot][b * nt:(b + 1) * nt, lanes(s)])
            else:
                xpad_ref[s, hist:hist + tm, :] = cin_ref[slot][:, lanes(s)]

        conv = None
        for j in range(CONV_W):
            term = cw_ref[j] * xpad_ref[:, j * nb:j * nb + tm, :]
            conv = term if conv is None else conv + term
        conv = cb_ref[...] + conv
        xc = jnp.concatenate([conv[s] for s in range(n_slab)], axis=1)

        tail = xpad_ref[:, tm:tm + hist, :]
        convn_ref[...] = tail
        xpad_ref[:, 0:hist, :] = tail
        yield

        lam = lam_ref[...]
        softplus_neg_lam = jnp.maximum(-lam, 0.0) + jnp.log1p(jnp.exp(-jnp.abs(lam)))
        c2 = (-0.5 * LRU_C * math.log2(math.e)) * softplus_neg_lam
        for jt, start in enumerate(gate_windows):
            cols = slice(jt * GATE_COLS, (jt + 1) * GATE_COLS)
            xw = xc[:, start:start + GATE_ROWS].astype(BF16)
            xh = xc[:, cols]
            th_a = jnp.tanh(_mm(xw, wbd_ref[0, jt]) + 0.5 * ba_ref[:, cols])
            th_i = jnp.tanh(_mm(xw, wbd_ref[1, jt]) + 0.5 * bi_ref[:, cols])
            a = jnp.exp2(c2[:, cols] * th_a + c2[:, cols])
            hx = 0.5 * xh
            y1 = 1.0 - a * a
            u = (y1 * lax.rsqrt(jnp.maximum(y1, RSQRT_FLOOR))) * (hx * th_i + hx)
            for s in range(GATE_COLS // LANES):
                a_ref[jt * (GATE_COLS // LANES) + s] = a[:, lanes(s)]
                u_ref[jt * (GATE_COLS // LANES) + s] = u[:, lanes(s)]
            yield

        h = h_ref[...]
        for t in range(nt):
            rows = slice(t * nb, (t + 1) * nb)
            h = a_ref[:, rows, :] * h + u_ref[:, rows, :]
            u_ref[:, rows, :] = h
        h_ref[...] = h
        hlast_ref[...] = h
        yield

        if batch_major:
            hs = jnp.concatenate(
                [jnp.concatenate([u_ref[s, pl.ds(b, nt, stride=nb), :] for b in range(nb)], axis=0)
                 for s in range(n_slab)], axis=1)
        else:
            hs = jnp.concatenate([u_ref[s] for s in range(n_slab)], axis=1)
        y = (gate_ref[slot][...] * hs).astype(BF16)
        x = x_ref[...].reshape(tm, x_ref.shape[-1])
        out = _layer_norm(ALPHA * x + _mm(y, wout_ref[...]), g_ref[...], b_ref[...])
        o_ref[...] = out.reshape(o_ref.shape)
        yield

    if pipelined:
        for slot in range(2):
            @pl.when(step % 2 == slot)
            def _(slot=slot):
                _interleave(recurrent(slot), in_proj(xn_ref, 1 - slot))
    else:
        _interleave(recurrent(0))


def _interleave(main, side=(), side_per_phase=2):
    side = iter(side)
    for _ in main:
        for _ in range(side_per_phase):
            next(side, None)
    for _ in side:
        pass


def _rglru(x, w, j, conv0, h0, ln, ln_idx, *, nb):
    w_in, cw, cb, wbd, ba, bi, lam, w_out, gate_windows = w
    ln_g, ln_b = ln
    dr, d = w_out.shape[1:]
    n_slab = dr // LANES
    hist = (CONV_W - 1) * nb
    if x.ndim == 3:
        nt = ROW_TILE // nb
        tm = ROW_TILE
        steps = x.shape[1] // nt
        block = (nb, nt, d)
        at = lambda i: (0, i, 0)
    else:
        tm = min(ROW_TILE, x.shape[0])
        steps = x.shape[0] // tm
        block = (tm, d)
        at = lambda i: (i, 0)
    assert tm % nb == 0 and tm >= hist and nb % SUBLANES == 0
    pipelined = steps > 1
    x_spec = pl.BlockSpec(block, at)
    x_specs, xs = [x_spec], [x]
    if pipelined:
        x_specs.append(pl.BlockSpec(block, lambda i: at(jnp.minimum(i + 1, steps - 1))))
        xs.append(x)
    vmem = ((2 + 2 * len(xs)) * _nbytes((tm, d), F32) + _nbytes(w_in.shape[1:], BF16)
            + _nbytes(wbd.shape[1:], BF16) + _nbytes(w_out.shape[1:], BF16)
            + 4 * _nbytes((hist + nb, dr), F32) + _nbytes((tm + hist, dr), F32)
            + 6 * _nbytes((tm, dr), F32) + _nbytes((nb, dr), F32) + 8 * _nbytes((tm, dr), F32))
    sl = _resident_slice
    return pl.pallas_call(
        functools.partial(_rglru_kernel, nb=nb, pipelined=pipelined, gate_windows=gate_windows),
        grid=(steps,),
        in_specs=x_specs + [sl(w_in.shape, j), sl(cw.shape, j), sl(cb.shape, j), sl(wbd.shape, j),
                            sl(ba.shape, j), sl(bi.shape, j), sl(lam.shape, j),
                            _resident(conv0.shape), _resident(h0.shape),
                            sl(w_out.shape, j), sl(ln_g.shape, ln_idx), sl(ln_b.shape, ln_idx)],
        out_specs=[x_spec,
                   pl.BlockSpec((n_slab, hist, LANES), lambda i: (0, 0, 0)),
                   pl.BlockSpec((n_slab, nb, LANES), lambda i: (0, 0, 0))],
        out_shape=[jax.ShapeDtypeStruct(x.shape, F32),
                   jax.ShapeDtypeStruct((n_slab, hist, LANES), F32),
                   jax.ShapeDtypeStruct((n_slab, nb, LANES), F32)],
        scratch_shapes=[pltpu.VMEM((tm, dr), F32), pltpu.VMEM((tm, dr), F32),
                        pltpu.VMEM((tm, dr), F32), pltpu.VMEM((tm, dr), F32),
                        pltpu.VMEM((n_slab, tm + hist, LANES), F32),
                        pltpu.VMEM((n_slab, tm, LANES), F32),
                        pltpu.VMEM((n_slab, tm, LANES), F32),
                        pltpu.VMEM((n_slab, nb, LANES), F32)],
        compiler_params=_params(("arbitrary",), vmem),
        name="rglru",
    )(*xs, w_in, cw, cb, wbd, ba, bi, lam, conv0, h0, w_out, ln_g, ln_b)


def _block_diag(blocks):
    n, bs, _ = blocks.shape
    cols = [jnp.pad(blocks[i], ((i * bs, (n - 1 - i) * bs), (0, 0))) for i in range(n)]
    return jnp.concatenate(cols, axis=1)


def _gate_windows(dr, block):
    starts = []
    for c0 in range(0, dr, GATE_COLS):
        lo = c0 // block * block
        hi = ((c0 + GATE_COLS - 1) // block + 1) * block
        start = min(lo // LANES * LANES, dr - GATE_ROWS)
        assert start <= lo and hi <= start + GATE_ROWS
        starts.append(start)
    return tuple(starts)


def _prep_rec(w_in, conv_w, conv_b, w_a, b_a, w_i, b_i, lam, w_out):
    n_layers, dr, _ = w_out.shape
    n_slab = dr // LANES
    windows = _gate_windows(dr, w_a.shape[2])
    wbd = jnp.stack([
        jnp.stack([
            jnp.stack([_block_diag(w[j])[start:start + GATE_ROWS, jt * GATE_COLS:(jt + 1) * GATE_COLS]
                       for jt, start in enumerate(windows)])
            for w in (w_a, w_i)])
        for j in range(n_layers)])
    wbd = (0.5 * wbd).astype(BF16)
    row = lambda v: v.reshape(n_layers, 1, dr)
    return (w_in.astype(BF16), conv_w.reshape(n_layers, CONV_W, n_slab, 1, LANES),
            conv_b.reshape(n_layers, n_slab, 1, LANES), wbd, row(b_a), row(b_i), row(lam),
            w_out.astype(BF16), windows)


def _to_slabs(a):
    rows, dr = a.shape
    return a.reshape(rows, dr // LANES, LANES).transpose(1, 0, 2)


def _from_slabs(a):
    n_slab, rows, _ = a.shape
    return a.transpose(1, 0, 2).reshape(rows, n_slab * LANES)


def _rope_tables(pos, dk):
    half = dk // 2
    inv = ROPE_BASE ** (-jnp.arange(half, dtype=F32) / half)
    ang = pos.astype(F32)[:, None] * inv[None, :]
    return jnp.cos(ang), jnp.sin(ang)


def _to_time_major(a, b, t):
    return a.reshape(b, t, a.shape[-1]).transpose(1, 0, 2).reshape(t * b, a.shape[-1])


def _to_batch_major(a, b, t):
    return a.reshape(t, b, a.shape[-1]).transpose(1, 0, 2).reshape(b * t, a.shape[-1])


def _trunk(x, pos, state_ret, state_conv, state_lru, ln, ffn1, ffn2, ret, rec):
    b, t, d = x.shape
    m = b * t
    x = x.reshape(m, d)
    long_seq = t >= ROW_TILE
    new_ret, new_conv, new_lru = [], [], []
    for layer in range(DEPTH):
        j = layer // 2
        x = yield (x, ffn1, layer, 3 * layer)
        if layer % 2 == 0:
            w_proj, w_out, gn_g = ret
            vd, qk = w_out.shape[1:]
            cos, sin = _rope_tables(pos, qk // RET_HEADS)
            chunk = math.gcd(t, RET_CHUNK)
            q, k, v, g = _ret_proj(x, w_proj, j, cos, sin, qk, vd)
            qkvg = (q.reshape(b, t, qk), k.reshape(b, t, qk), v.reshape(b, t, vd), g.reshape(b, t, vd))
            if state_ret is None:
                assert long_seq
                x, s_new = _ret_fused(*qkvg, x.reshape(b, t, d), gn_g, w_out, j, ln, 3 * layer + 1,
                                      tt=ROW_TILE, chunk=math.gcd(ROW_TILE, RET_LONG_CHUNK))
                x = x.reshape(m, d)
            else:
                assert not long_seq
                gated, s_new = _ret_core(*qkvg, gn_g, j, state_ret,
                                         bb=SHORT_SEQ_BATCH_BLOCK, tt=t, chunk=chunk)
                x = _proj_ln(gated.reshape(m, vd), x, w_out, j, ln, 3 * layer + 1)
            new_ret.append(s_new)
        else:
            dr = rec[-2].shape[1]
            if state_conv is None:
                conv0 = jnp.zeros((dr // LANES, (CONV_W - 1) * b, LANES), F32)
                h0 = jnp.zeros((dr // LANES, b, LANES), F32)
            else:
                conv0 = _to_slabs(state_conv[j].transpose(1, 0, 2).reshape((CONV_W - 1) * b, dr))
                h0 = _to_slabs(state_lru[j])
            x_in = x.reshape(b, t, d) if long_seq else _to_time_major(x, b, t)
            x_out, conv_n, h_last = _rglru(x_in, rec, j, conv0, h0, ln, 3 * layer + 1, nb=b)
            x = x_out.reshape(m, d) if long_seq else _to_batch_major(x_out, b, t)
            new_conv.append(_from_slabs(conv_n).reshape(CONV_W - 1, b, dr).transpose(1, 0, 2))
            new_lru.append(_from_slabs(h_last))
        x = yield (x, ffn2, layer, 3 * layer + 2)
    return x.reshape(b, t, d), jnp.stack(new_ret), jnp.stack(new_conv), jnp.stack(new_lru)


def _run_trunks(trunks, ln):
    requests = [next(g) for g in trunks]
    results = [None] * len(trunks)
    while results[0] is None:
        _, w, layer, ln_idx = requests[0]
        outs = _ffn_ln([r[0] for r in requests], w, layer, ln, ln_idx)
        for i, g in enumerate(trunks):
            try:
                requests[i] = g.send(outs[i])
            except StopIteration as done:
                results[i] = done.value
    return results


def kernel(x_prompt, x_sample, state_ret, state_conv, state_lru, ln_g, ln_b, ffn1_w_in, ffn1_w_out,
           ffn2_w_in, ffn2_w_out, ret_w_in, ret_gn_g, ret_w_out, rec_w_in, rec_conv_w, rec_conv_b,
           rec_w_a, rec_b_a, rec_w_i, rec_b_i, rec_lam, rec_w_out):
    cast = lambda w: w.astype(BF16)
    d = ln_g.shape[-1]
    ln = (ln_g.reshape(-1, 1, d), ln_b.reshape(-1, 1, d))
    ffn1 = (cast(ffn1_w_in), cast(ffn1_w_out))
    ffn2 = (cast(ffn2_w_in), cast(ffn2_w_out))
    ret = (cast(ret_w_in), cast(ret_w_out), ret_gn_g.reshape(ret_gn_g.shape[0], 1, -1))
    rec = _prep_rec(rec_w_in, rec_conv_w, rec_conv_b, rec_w_a, rec_b_a, rec_w_i, rec_b_i, rec_lam,
                    rec_w_out)
    weights = (ln, ffn1, ffn2, ret, rec)
    tp, ts = x_prompt.shape[1], x_sample.shape[1]
    prompt = _trunk(x_prompt, jnp.arange(tp, dtype=jnp.int32), None, None, None, *weights)
    sample = _trunk(x_sample, PAST_LEN + jnp.arange(ts, dtype=jnp.int32),
                    state_ret, state_conv, state_lru, *weights)
    (y_p, ret_p, conv_p, lru_p), (y_s, ret_s, conv_s, lru_s) = _run_trunks([prompt, sample], ln)
    return (y_p, y_s, ret_p, conv_p, lru_p, ret_s, conv_s, lru_s)
```

```python
import functools
import math

import numpy as np
import jax
import jax.numpy as jnp
from jax import lax
from jax.experimental import pallas as pl
from jax.experimental.pallas import tpu as pltpu

F32 = jnp.float32
BF16 = jnp.bfloat16

DEPTH = 2
PAST_LEN = 16384
RET_HEADS = 4
RET_CHUNK = 128
RET_LONG_CHUNK = 256
ROPE_BASE = 10000.0
CONV_W = 4
LRU_C = 8.0
ALPHA = (2.0 * DEPTH) ** 0.25
LN_EPS = 1e-5
GN_EPS = 1e-6
RSQRT_FLOOR = 1e-30

V7X_SCOPED_VMEM_BYTES = 60000 * 1024
LANES = 128
SUBLANES = 8
ROW_TILE = 512
FFN_ROW_TILE = 512
LN_ROWS = 128
FF_CHUNK = 256
PROJ_CHUNK = 256
GATE_COLS = 256
GATE_ROWS = 512
SHORT_SEQ_BATCH_BLOCK = 4


def _params(semantics, vmem_bytes):
    return pltpu.CompilerParams(
        dimension_semantics=semantics,
        vmem_limit_bytes=int(min(V7X_SCOPED_VMEM_BYTES, vmem_bytes)),
    )


def _resident(shape):
    zeros = (0,) * len(shape)
    return pl.BlockSpec(shape, lambda *_: zeros, pipeline_mode=pl.Buffered(1))


def _resident_slice(shape, idx):
    index = (idx,) + (0,) * (len(shape) - 1)
    return pl.BlockSpec((None,) + tuple(shape[1:]), lambda *_: index, pipeline_mode=pl.Buffered(1))


def _nbytes(shape, dtype):
    return int(np.prod(shape)) * jnp.dtype(dtype).itemsize


def _layer_norm(y, g, b):
    mu = jnp.mean(y, axis=-1, keepdims=True)
    d = y - mu
    var = jnp.mean(d * d, axis=-1, keepdims=True)
    return d * lax.rsqrt(var + LN_EPS) * g + b


def _mm(a, b):
    return jnp.dot(a, b, preferred_element_type=F32)


def _ffn_ln_kernel(*refs, ff, first_tile):
    n = len(first_tile) - 1
    x_refs, (win_ref, wout_ref, g_ref, b_ref) = refs[:n], refs[n:n + 4]
    o_refs, y_refs = refs[n + 4:2 * n + 4], refs[2 * n + 4:]
    step = pl.program_id(0)

    def matmuls(x_ref, slot):
        tm = x_ref.shape[0]
        x = x_ref[...]
        xb = x.astype(BF16)
        acc = jnp.zeros(x.shape, F32)
        for lo in range(0, ff, FF_CHUNK):
            gate = _mm(xb, win_ref[:, lo:lo + FF_CHUNK])
            up = _mm(xb, win_ref[:, ff + lo:ff + lo + FF_CHUNK])
            h = (gate * jax.nn.sigmoid(gate) * up).astype(BF16)
            acc = acc + _mm(h, wout_ref[lo:lo + FF_CHUNK, :])
            yield
        y_refs[slot][0:tm, :] = ALPHA * x + 0.5 * acc
        yield

    def norm(o_ref, slot):
        piece = math.gcd(o_ref.shape[0], LN_ROWS)
        for lo in range(0, o_ref.shape[0], piece):
            rows = slice(lo, lo + piece)
            o_ref[rows, :] = _layer_norm(y_refs[slot][rows, :], g_ref[...], b_ref[...])
            yield

    def array_of(tile):
        return next(i for i in range(n) if first_tile[i] <= tile < first_tile[i + 1])

    total = first_tile[-1]
    runs = {}
    for s in range(total + 1):
        key = (array_of(s) if s < total else None, array_of(s - 1) if s > 0 else None, s % 2)
        lo, hi = runs.get(key, (s, s))
        runs[key] = (min(lo, s), max(hi, s))
    for (cur, prev, slot), (lo, hi) in runs.items():
        @pl.when((step >= lo) & (step <= hi) & (step % 2 == slot))
        def _(cur=cur, prev=prev, slot=slot):
            main = matmuls(x_refs[cur], slot) if cur is not None else ()
            side = norm(o_refs[prev], 1 - slot) if prev is not None else ()
            _interleave(iter(main), side, side_per_phase=1)


def _ffn_ln(xs, w, layer, ln, ln_idx):
    w_in, w_out = w
    ln_g, ln_b = ln
    d = xs[0].shape[1]
    ff = w_out.shape[1]
    assert ff % FF_CHUNK == 0
    tms = [min(FFN_ROW_TILE, x.shape[0]) for x in xs]
    first_tile = [0]
    for x, tm in zip(xs, tms):
        first_tile.append(first_tile[-1] + x.shape[0] // tm)

    def rows(i, lag):
        lo, n_tiles = first_tile[i], first_tile[i + 1] - first_tile[i]
        return pl.BlockSpec((tms[i], d), lambda s: (jnp.clip(s - lag - lo, 0, n_tiles - 1), 0))

    n = len(xs)
    vmem = (sum(4 * _nbytes((tm, d), F32) for tm in tms) + _nbytes(w_in.shape[1:], BF16)
            + _nbytes(w_out.shape[1:], BF16) + 10 * _nbytes((max(tms), d), F32))
    return pl.pallas_call(
        functools.partial(_ffn_ln_kernel, ff=ff, first_tile=tuple(first_tile)),
        grid=(first_tile[-1] + 1,),
        in_specs=[rows(i, 0) for i in range(n)]
        + [_resident_slice(w_in.shape, layer), _resident_slice(w_out.shape, layer),
           _resident_slice(ln_g.shape, ln_idx), _resident_slice(ln_b.shape, ln_idx)],
        out_specs=[rows(i, 1) for i in range(n)],
        out_shape=[jax.ShapeDtypeStruct(x.shape, F32) for x in xs],
        scratch_shapes=[pltpu.VMEM((max(tms), d), F32), pltpu.VMEM((max(tms), d), F32)],
        compiler_params=_params(("arbitrary",), vmem),
        name="ffn_ln",
    )(*xs, w_in, w_out, ln_g, ln_b)


def _proj_ln_kernel(a_ref, x_ref, w_ref, g_ref, b_ref, o_ref):
    y = ALPHA * x_ref[...] + _mm(a_ref[...], w_ref[...])
    o_ref[...] = _layer_norm(y, g_ref[...], b_ref[...])


def _proj_ln(a, x, w, j, ln, ln_idx):
    ln_g, ln_b = ln
    m, d = x.shape
    k = a.shape[1]
    tm = min(ROW_TILE, m)
    vmem = (2 * _nbytes((tm, k), BF16) + 4 * _nbytes((tm, d), F32) + _nbytes(w.shape[1:], BF16)
            + 4 * _nbytes((tm, d), F32))
    return pl.pallas_call(
        _proj_ln_kernel,
        grid=(m // tm,),
        in_specs=[pl.BlockSpec((tm, k), lambda i: (i, 0)), pl.BlockSpec((tm, d), lambda i: (i, 0)),
                  _resident_slice(w.shape, j),
                  _resident_slice(ln_g.shape, ln_idx), _resident_slice(ln_b.shape, ln_idx)],
        out_specs=pl.BlockSpec((tm, d), lambda i: (i, 0)),
        out_shape=jax.ShapeDtypeStruct((m, d), F32),
        compiler_params=_params(("parallel",), vmem),
        name="proj_ln",
    )(a, x, w, ln_g, ln_b)


def _ret_proj_kernel(x_ref, w_ref, cos_ref, sin_ref, q_ref, k_ref, v_ref, g_ref, *, heads, k_scale):
    xb = x_ref[...].astype(BF16)
    cos = cos_ref[...]
    sin = sin_ref[...]
    half = cos.shape[1]
    qk = q_ref.shape[1]
    vd = v_ref.shape[1]
    for col, dst, scale in ((0, q_ref, None), (qk, k_ref, k_scale)):
        p = _mm(xb, w_ref[:, col:col + qk])
        for h in range(heads):
            lo = 2 * half * h
            x1 = p[:, lo:lo + half]
            x2 = p[:, lo + half:lo + 2 * half]
            r1 = x1 * cos - x2 * sin
            r2 = x1 * sin + x2 * cos
            if scale is not None:
                r1 = r1 * scale
                r2 = r2 * scale
            dst[:, lo:lo + half] = r1.astype(BF16)
            dst[:, lo + half:lo + 2 * half] = r2.astype(BF16)
    v_ref[...] = _mm(xb, w_ref[:, 2 * qk:2 * qk + vd]).astype(BF16)
    g_ref[...] = _mm(xb, w_ref[:, 2 * qk + vd:]).astype(BF16)


def _ret_proj(x, w, j, cos, sin, qk, vd):
    m, d = x.shape
    tm = min(ROW_TILE, m)
    if cos.shape[0] < tm:
        reps = tm // cos.shape[0]
        cos, sin = jnp.tile(cos, (reps, 1)), jnp.tile(sin, (reps, 1))
    n_tab = cos.shape[0] // tm
    row = lambda n: pl.BlockSpec((tm, n), lambda i: (i, 0))
    tab = pl.BlockSpec((tm, cos.shape[1]), lambda i: (i % n_tab, 0))
    vmem = (2 * _nbytes((tm, d), F32) + _nbytes(w.shape[1:], BF16)
            + 4 * (_nbytes((tm, qk), BF16) + _nbytes((tm, vd), BF16))
            + 4 * _nbytes((tm, vd), F32))
    dk = qk // RET_HEADS
    return pl.pallas_call(
        functools.partial(_ret_proj_kernel, heads=RET_HEADS, k_scale=dk ** -0.5),
        grid=(m // tm,),
        in_specs=[row(d), _resident_slice(w.shape, j), tab, tab],
        out_specs=[row(qk), row(qk), row(vd), row(vd)],
        out_shape=[jax.ShapeDtypeStruct((m, qk), BF16), jax.ShapeDtypeStruct((m, qk), BF16),
                   jax.ShapeDtypeStruct((m, vd), BF16), jax.ShapeDtypeStruct((m, vd), BF16)],
        compiler_params=_params(("parallel",), vmem),
        name="ret_proj",
    )(x, w, cos, sin)


def _retention_chunk(qc, kc, vc, s, dmask, q_dec, k_dec, chunk_dec):
    scores = lax.dot_general(qc, kc, (((1,), (1,)), ((), ())), preferred_element_type=F32) * dmask
    o = _mm(scores.astype(BF16), vc) + _mm((qc * q_dec).astype(BF16), s.astype(BF16))
    s_new = s * chunk_dec + lax.dot_general((kc * k_dec).astype(BF16), vc, (((0,), (0,)), ((), ())),
                                            preferred_element_type=F32)
    return o, s_new


def _norm_gate(o, gate, gn):
    mu = jnp.mean(o, axis=-1, keepdims=True)
    d = o - mu
    var = jnp.mean(d * d, axis=-1, keepdims=True)
    on = d * lax.rsqrt(var + GN_EPS) * gn
    gate = gate.astype(F32)
    return (gate * jax.nn.sigmoid(gate) * on).astype(BF16)


def _ret_core_kernel(q_ref, k_ref, v_ref, g_ref, gn_ref, dm_ref, qd_ref, kd_ref, cd_ref, s0_ref,
                     o_ref, s_ref, *, heads, chunk):
    bb, tt, _ = q_ref.shape
    dk = q_ref.shape[2] // heads
    dv = v_ref.shape[2] // heads

    @pl.when(pl.program_id(1) == 0)
    def _():
        s_ref[...] = s0_ref[...]

    for b in range(bb):
        for c in range(tt // chunk):
            for h in range(heads):
                rows = slice(c * chunk, (c + 1) * chunk)
                qk_cols = slice(h * dk, (h + 1) * dk)
                v_cols = slice(h * dv, (h + 1) * dv)
                o, s_ref[b, h] = _retention_chunk(
                    q_ref[b, rows, qk_cols], k_ref[b, rows, qk_cols], v_ref[b, rows, v_cols],
                    s_ref[b, h], dm_ref[h], qd_ref[h], kd_ref[h], cd_ref[h])
                o_ref[b, rows, v_cols] = _norm_gate(o, g_ref[b, rows, v_cols], gn_ref[:, v_cols])


def _ret_fused_kernel(q_ref, k_ref, v_ref, g_ref, x_ref, gn_ref, dm_ref, qd_ref, kd_ref, cd_ref,
                      wout_ref, lng_ref, lnb_ref, o_ref, s_ref, ret0_ref, ret1_ref, gated_ref,
                      *, heads, chunk, tiles_per_row, n_tiles):
    ret_ref = (ret0_ref, ret1_ref)
    tt = q_ref.shape[1]
    dk = q_ref.shape[2] // heads
    dv = v_ref.shape[2] // heads
    d_model = o_ref.shape[2]
    step = pl.program_id(0)
    chunks = [slice(c * chunk, (c + 1) * chunk) for c in range(tt // chunk)]
    v_cols = lambda h: slice(h * dv, (h + 1) * dv)

    @pl.when((step % tiles_per_row == 0) & (step < n_tiles))
    def _():
        s_ref[...] = jnp.zeros(s_ref.shape, F32)

    def attend(slot):
        for rows in chunks:
            for h in range(heads):
                qk_cols = slice(h * dk, (h + 1) * dk)
                ret_ref[slot][rows, v_cols(h)], s_ref[0, h] = _retention_chunk(
                    q_ref[0, rows, qk_cols], k_ref[0, rows, qk_cols], v_ref[0, rows, v_cols(h)],
                    s_ref[0, h], dm_ref[h], qd_ref[h], kd_ref[h], cd_ref[h])
                yield

    def finish(slot):
        for h in range(heads):
            for rows in chunks:
                gated_ref[rows, v_cols(h)] = _norm_gate(
                    ret_ref[slot][rows, v_cols(h)], g_ref[0, rows, v_cols(h)], gn_ref[:, v_cols(h)])
                yield
            for lo in range(0, d_model, PROJ_CHUNK):
                cols = slice(lo, lo + PROJ_CHUNK)
                part = _mm(gated_ref[:, v_cols(h)], wout_ref[v_cols(h), cols])
                o_ref[0, :, cols] = part if h == 0 else o_ref[0, :, cols] + part
                yield
        o_ref[0] = _layer_norm(ALPHA * x_ref[0] + o_ref[0], lng_ref[...], lnb_ref[...])
        yield

    @pl.when(step == 0)
    def _():
        _interleave(attend(0))

    for slot in range(2):
        @pl.when((step > 0) & (step < n_tiles) & (step % 2 == slot))
        def _(slot=slot):
            _interleave(attend(slot), finish(1 - slot))

    @pl.when(step == n_tiles)
    def _():
        _interleave(finish((n_tiles - 1) % 2))


def _decay_tables(chunk):
    lg = jnp.log1p(-jnp.exp2(-5.0 - jnp.arange(RET_HEADS, dtype=F32)))
    idx = jnp.arange(chunk, dtype=F32)
    rel = idx[:, None] - idx[None, :]
    dmask = jnp.where(rel >= 0, jnp.exp(lg[:, None, None] * jnp.maximum(rel, 0.0)), 0.0)
    q_dec = jnp.exp(lg[:, None] * (idx + 1.0))[:, :, None]
    k_dec = jnp.exp(lg[:, None] * (chunk - 1.0 - idx))[:, :, None]
    chunk_dec = jnp.exp(lg * chunk)[:, None, None]
    return dmask, q_dec, k_dec, chunk_dec


def _ret_core(q, k, v, g, gn_g, j, s0, *, bb, tt, chunk):
    b, t, qk = q.shape
    vd = v.shape[2]
    heads = RET_HEADS
    dk, dv = qk // heads, vd // heads
    tables = _decay_tables(chunk)
    blk = lambda n: pl.BlockSpec((bb, tt, n), lambda i, jj: (i, jj, 0))
    vmem = (4 * (_nbytes((bb, tt, qk), BF16) + _nbytes((bb, tt, vd), BF16))
            + 2 * _nbytes((bb, tt, vd), BF16) + 4 * _nbytes((bb, heads, dk, dv), F32)
            + 16 * _nbytes((max(chunk, SUBLANES), dv), F32) + 8 * _nbytes((dk, dv), F32))
    return pl.pallas_call(
        functools.partial(_ret_core_kernel, heads=heads, chunk=chunk),
        grid=(b // bb, t // tt),
        in_specs=[blk(qk), blk(qk), blk(vd), blk(vd), _resident_slice(gn_g.shape, j)]
        + [_resident(tab.shape) for tab in tables]
        + [pl.BlockSpec((None, bb, heads, dk, dv), lambda i, jj: (j, i, 0, 0, 0))],
        out_specs=[blk(vd), pl.BlockSpec((bb, heads, dk, dv), lambda i, jj: (i, 0, 0, 0))],
        out_shape=[jax.ShapeDtypeStruct((b, t, vd), BF16),
                   jax.ShapeDtypeStruct((b, heads, dk, dv), F32)],
        compiler_params=_params(("parallel", "arbitrary"), vmem),
        name="ret_core",
    )(q, k, v, g, gn_g, *tables, s0)


def _ret_fused(q, k, v, g, x, gn_g, w_out, j, ln, ln_idx, *, tt, chunk):
    ln_g, ln_b = ln
    b, t, qk = q.shape
    vd, d = v.shape[2], x.shape[2]
    heads = RET_HEADS
    dk, dv = qk // heads, vd // heads
    tables = _decay_tables(chunk)
    tiles_per_row = t // tt
    n_tiles = b * tiles_per_row
    assert tiles_per_row >= 2

    def tile_block(n, lag):
        def index(i):
            tile = jnp.clip(i - lag, 0, n_tiles - 1)
            return (tile // tiles_per_row, tile % tiles_per_row, 0)
        return pl.BlockSpec((1, tt, n), index)

    state = pl.BlockSpec((1, heads, dk, dv),
                         lambda i: (jnp.minimum(i, n_tiles - 1) // tiles_per_row, 0, 0, 0))
    vmem = (4 * (_nbytes((tt, qk), BF16) + _nbytes((tt, vd), BF16)) + 2 * _nbytes((tt, vd), BF16)
            + 4 * _nbytes((tt, d), F32) + 2 * _nbytes((heads, dk, dv), F32)
            + _nbytes(w_out.shape[1:], BF16) + 2 * _nbytes((tt, vd), F32) + _nbytes((tt, vd), BF16)
            + 16 * _nbytes((chunk, dv), F32) + 8 * _nbytes((dk, dv), F32) + 4 * _nbytes((tt, d), F32))
    return pl.pallas_call(
        functools.partial(_ret_fused_kernel, heads=heads, chunk=chunk, tiles_per_row=tiles_per_row,
                          n_tiles=n_tiles),
        grid=(n_tiles + 1,),
        in_specs=[tile_block(qk, 0), tile_block(qk, 0), tile_block(vd, 0), tile_block(vd, 1),
                  tile_block(d, 1), _resident_slice(gn_g.shape, j)]
        + [_resident(tab.shape) for tab in tables]
        + [_resident_slice(w_out.shape, j), _resident_slice(ln_g.shape, ln_idx),
           _resident_slice(ln_b.shape, ln_idx)],
        out_specs=[tile_block(d, 1), state],
        out_shape=[jax.ShapeDtypeStruct((b, t, d), F32),
                   jax.ShapeDtypeStruct((b, heads, dk, dv), F32)],
        scratch_shapes=[pltpu.VMEM((tt, vd), F32), pltpu.VMEM((tt, vd), F32),
                        pltpu.VMEM((tt, vd), BF16)],
        compiler_params=_params(("arbitrary",), vmem),
        name="ret_fused",
    )(q, k, v, g, x, gn_g, *tables, w_out, ln_g, ln_b)


def _gelu_tanh(x):
    c = math.sqrt(2.0 / math.pi)
    th = jnp.tanh(x * ((x * x) * (c * 0.044715) + c))
    hx = 0.5 * x
    return hx * th + hx


def _rglru_kernel(*refs, nb, pipelined, gate_windows):
    if pipelined:
        x_ref, xn_ref = refs[:2]
        refs = refs[2:]
    else:
        x_ref, xn_ref = refs[0], None
        refs = refs[1:]
    (win_ref, cw_ref, cb_ref, wbd_ref, ba_ref, bi_ref, lam_ref, conv0_ref, h0_ref, wout_ref,
     g_ref, b_ref, o_ref, convn_ref, hlast_ref,
     gate0_ref, gate1_ref, cin0_ref, cin1_ref, xpad_ref, a_ref, u_ref, h_ref) = refs
    gate_ref = (gate0_ref, gate1_ref)
    cin_ref = (cin0_ref, cin1_ref)
    batch_major = len(x_ref.shape) == 3
    dr = wout_ref.shape[0]
    n_slab = dr // LANES
    hist = (CONV_W - 1) * nb
    tm = a_ref.shape[1]
    nt = tm // nb
    lanes = lambda s: slice(s * LANES, (s + 1) * LANES)
    step = pl.program_id(0)

    def in_proj(src_ref, slot):
        xb = src_ref[...].reshape(tm, src_ref.shape[-1]).astype(BF16)
        for lo in range(0, dr, PROJ_CHUNK):
            cin_ref[slot][:, lo:lo + PROJ_CHUNK] = _mm(xb, win_ref[:, dr + lo:dr + lo + PROJ_CHUNK])
            yield
        for lo in range(0, dr, PROJ_CHUNK):
            gate_ref[slot][:, lo:lo + PROJ_CHUNK] = _gelu_tanh(_mm(xb, win_ref[:, lo:lo + PROJ_CHUNK]))
            yield

    @pl.when(step == 0)
    def _():
        xpad_ref[:, 0:hist, :] = conv0_ref[...]
        h_ref[...] = h0_ref[...]
        _interleave(in_proj(x_ref, 0))

    def recurrent(slot):
        for s in range(n_slab):
            if batch_major:
                for b in range(nb):
                    xpad_ref[s, pl.ds(hist + b, nt, stride=nb), :] = (
                        cin_ref[slot][b * nt:(b + 1) * nt, lanes(s)])
            else:
                xpad_ref[s, hist:hist + tm, :] = cin_ref[slot][:, lanes(s)]

        conv = None
        for j in range(CONV_W):
            term = cw_ref[j] * xpad_ref[:, j * nb:j * nb + tm, :]
            conv = term if conv is None else conv + term
        conv = cb_ref[...] + conv
        xc = jnp.concatenate([conv[s] for s in range(n_slab)], axis=1)

        tail = xpad_ref[:, tm:tm + hist, :]
        convn_ref[...] = tail
        xpad_ref[:, 0:hist, :] = tail
        yield

        lam = lam_ref[...]
        softplus_neg_lam = jnp.maximum(-lam, 0.0) + jnp.log1p(jnp.exp(-jnp.abs(lam)))
        c2 = (-0.5 * LRU_C * math.log2(math.e)) * softplus_neg_lam
        for jt, start in enumerate(gate_windows):
            cols = slice(jt * GATE_COLS, (jt + 1) * GATE_COLS)
            xw = xc[:, start:start + GATE_ROWS].astype(BF16)
            xh = xc[:, cols]
            th_a = jnp.tanh(_mm(xw, wbd_ref[0, jt]) + 0.5 * ba_ref[:, cols])
            th_i = jnp.tanh(_mm(xw, wbd_ref[1, jt]) + 0.5 * bi_ref[:, cols])
            a = jnp.exp2(c2[:, cols] * th_a + c2[:, cols])
            hx = 0.5 * xh
            y1 = 1.0 - a * a
            u = (y1 * lax.rsqrt(jnp.maximum(y1, RSQRT_FLOOR))) * (hx * th_i + hx)
            for s in range(GATE_COLS // LANES):
                a_ref[jt * (GATE_COLS // LANES) + s] = a[:, lanes(s)]
                u_ref[jt * (GATE_COLS // LANES) + s] = u[:, lanes(s)]
            yield

        h = h_ref[...]
        for t in range(nt):
            rows = slice(t * nb, (t + 1) * nb)
            h = a_ref[:, rows, :] * h + u_ref[:, rows, :]
            u_ref[:, rows, :] = h
        h_ref[...] = h
        hlast_ref[...] = h
        yield

        if batch_major:
            hs = jnp.concatenate(
                [jnp.concatenate([u_ref[s, pl.ds(b, nt, stride=nb), :] for b in range(nb)], axis=0)
                 for s in range(n_slab)], axis=1)
        else:
            hs = jnp.concatenate([u_ref[s] for s in range(n_slab)], axis=1)
        y = (gate_ref[slot][...] * hs).astype(BF16)
        x = x_ref[...].reshape(tm, x_ref.shape[-1])
        out = _layer_norm(ALPHA * x + _mm(y, wout_ref[...]), g_ref[...], b_ref[...])
        o_ref[...] = out.reshape(o_ref.shape)
        yield

    if pipelined:
        for slot in range(2):
            @pl.when(step % 2 == slot)
            def _(slot=slot):
                _interleave(recurrent(slot), in_proj(xn_ref, 1 - slot))
    else:
        _interleave(recurrent(0))


def _interleave(main, side=(), side_per_phase=2):
    side = iter(side)
    for _ in main:
        for _ in range(side_per_phase):
            next(side, None)
    for _ in side:
        pass


def _rglru(x, w, j, conv0, h0, ln, ln_idx, *, nb):
    w_in, cw, cb, wbd, ba, bi, lam, w_out, gate_windows = w
    ln_g, ln_b = ln
    dr, d = w_out.shape[1:]
    n_slab = dr // LANES
    hist = (CONV_W - 1) * nb
    if x.ndim == 3:
        nt = ROW_TILE // nb
        tm = ROW_TILE
        steps = x.shape[1] // nt
        block = (nb, nt, d)
        at = lambda i: (0, i, 0)
    else:
        tm = min(ROW_TILE, x.shape[0])
        steps = x.shape[0] // tm
        block = (tm, d)
        at = lambda i: (i, 0)
    assert tm % nb == 0 and tm >= hist and nb % SUBLANES == 0
    pipelined = steps > 1
    x_spec = pl.BlockSpec(block, at)
    x_specs, xs = [x_spec], [x]
    if pipelined:
        x_specs.append(pl.BlockSpec(block, lambda i: at(jnp.minimum(i + 1, steps - 1))))
        xs.append(x)
    vmem = ((2 + 2 * len(xs)) * _nbytes((tm, d), F32) + _nbytes(w_in.shape[1:], BF16)
            + _nbytes(wbd.shape[1:], BF16) + _nbytes(w_out.shape[1:], BF16)
            + 4 * _nbytes((hist + nb, dr), F32) + _nbytes((tm + hist, dr), F32)
            + 6 * _nbytes((tm, dr), F32) + _nbytes((nb, dr), F32) + 8 * _nbytes((tm, dr), F32))
    sl = _resident_slice
    return pl.pallas_call(
        functools.partial(_rglru_kernel, nb=nb, pipelined=pipelined, gate_windows=gate_windows),
        grid=(steps,),
        in_specs=x_specs + [sl(w_in.shape, j), sl(cw.shape, j), sl(cb.shape, j), sl(wbd.shape, j),
                            sl(ba.shape, j), sl(bi.shape, j), sl(lam.shape, j),
                            _resident(conv0.shape), _resident(h0.shape),
                            sl(w_out.shape, j), sl(ln_g.shape, ln_idx), sl(ln_b.shape, ln_idx)],
        out_specs=[x_spec,
                   pl.BlockSpec((n_slab, hist, LANES), lambda i: (0, 0, 0)),
                   pl.BlockSpec((n_slab, nb, LANES), lambda i: (0, 0, 0))],
        out_shape=[jax.ShapeDtypeStruct(x.shape, F32),
                   jax.ShapeDtypeStruct((n_slab, hist, LANES), F32),
                   jax.ShapeDtypeStruct((n_slab, nb, LANES), F32)],
        scratch_shapes=[pltpu.VMEM((tm, dr), F32), pltpu.VMEM((tm, dr), F32),
                        pltpu.VMEM((tm, dr), F32), pltpu.VMEM((tm, dr), F32),
                        pltpu.VMEM((n_slab, tm + hist, LANES), F32),
                        pltpu.VMEM((n_slab, tm, LANES), F32),
                        pltpu.VMEM((n_slab, tm, LANES), F32),
                        pltpu.VMEM((n_slab, nb, LANES), F32)],
        compiler_params=_params(("arbitrary",), vmem),
        name="rglru",
    )(*xs, w_in, cw, cb, wbd, ba, bi, lam, conv0, h0, w_out, ln_g, ln_b)


def _block_diag(blocks):
    n, bs, _ = blocks.shape
    cols = [jnp.pad(blocks[i], ((i * bs, (n - 1 - i) * bs), (0, 0))) for i in range(n)]
    return jnp.concatenate(cols, axis=1)


def _gate_windows(dr, block):
    starts = []
    for c0 in range(0, dr, GATE_COLS):
        lo = c0 // block * block
        hi = ((c0 + GATE_COLS - 1) // block + 1) * block
        start = min(lo // LANES * LANES, dr - GATE_ROWS)
        assert start <= lo and hi <= start + GATE_ROWS
        starts.append(start)
    return tuple(starts)


def _prep_rec(w_in, conv_w, conv_b, w_a, b_a, w_i, b_i, lam, w_out):
    n_layers, dr, _ = w_out.shape
    n_slab = dr // LANES
    windows = _gate_windows(dr, w_a.shape[2])
    wbd = jnp.stack([
        jnp.stack([
            jnp.stack([_block_diag(w[j])[start:start + GATE_ROWS, jt * GATE_COLS:(jt + 1) * GATE_COLS]
                       for jt, start in enumerate(windows)])
            for w in (w_a, w_i)])
        for j in range(n_layers)])
    wbd = (0.5 * wbd).astype(BF16)
    row = lambda v: v.reshape(n_layers, 1, dr)
    return (w_in.astype(BF16), conv_w.reshape(n_layers, CONV_W, n_slab, 1, LANES),
            conv_b.reshape(n_layers, n_slab, 1, LANES), wbd, row(b_a), row(b_i), row(lam),
            w_out.astype(BF16), windows)


def _to_slabs(a):
    rows, dr = a.shape
    return a.reshape(rows, dr // LANES, LANES).transpose(1, 0, 2)


def _from_slabs(a):
    n_slab, rows, _ = a.shape
    return a.transpose(1, 0, 2).reshape(rows, n_slab * LANES)


def _rope_tables(pos, dk):
    half = dk // 2
    inv = ROPE_BASE ** (-jnp.arange(half, dtype=F32) / half)
    ang = pos.astype(F32)[:, None] * inv[None, :]
    return jnp.cos(ang), jnp.sin(ang)


def _to_time_major(a, b, t):
    return a.reshape(b, t, a.shape[-1]).transpose(1, 0, 2).reshape(t * b, a.shape[-1])


def _to_batch_major(a, b, t):
    return a.reshape(t, b, a.shape[-1]).transpose(1, 0, 2).reshape(b * t, a.shape[-1])


def _trunk(x, pos, state_ret, state_conv, state_lru, ln, ffn1, ffn2, ret, rec):
    b, t, d = x.shape
    m = b * t
    x = x.reshape(m, d)
    long_seq = t >= ROW_TILE
    new_ret, new_conv, new_lru = [], [], []
    for layer in range(DEPTH):
        j = layer // 2
        x = yield (x, ffn1, layer, 3 * layer)
        if layer % 2 == 0:
            w_proj, w_out, gn_g = ret
            vd, qk = w_out.shape[1:]
            cos, sin = _rope_tables(pos, qk // RET_HEADS)
            chunk = math.gcd(t, RET_CHUNK)
            q, k, v, g = _ret_proj(x, w_proj, j, cos, sin, qk, vd)
            qkvg = (q.reshape(b, t, qk), k.reshape(b, t, qk), v.reshape(b, t, vd), g.reshape(b, t, vd))
            if state_ret is None:
                assert long_seq
                x, s_new = _ret_fused(*qkvg, x.reshape(b, t, d), gn_g, w_out, j, ln, 3 * layer + 1,
                                      tt=ROW_TILE, chunk=math.gcd(ROW_TILE, RET_LONG_CHUNK))
                x = x.reshape(m, d)
            else:
                assert not long_seq
                gated, s_new = _ret_core(*qkvg, gn_g, j, state_ret,
                                         bb=SHORT_SEQ_BATCH_BLOCK, tt=t, chunk=chunk)
                x = _proj_ln(gated.reshape(m, vd), x, w_out, j, ln, 3 * layer + 1)
            new_ret.append(s_new)
        else:
            dr = rec[-2].shape[1]
            if state_conv is None:
                conv0 = jnp.zeros((dr // LANES, (CONV_W - 1) * b, LANES), F32)
                h0 = jnp.zeros((dr // LANES, b, LANES), F32)
            else:
                conv0 = _to_slabs(state_conv[j].transpose(1, 0, 2).reshape((CONV_W - 1) * b, dr))
                h0 = _to_slabs(state_lru[j])
            x_in = x.reshape(b, t, d) if long_seq else _to_time_major(x, b, t)
            x_out, conv_n, h_last = _rglru(x_in, rec, j, conv0, h0, ln, 3 * layer + 1, nb=b)
            x = x_out.reshape(m, d) if long_seq else _to_batch_major(x_out, b, t)
            new_conv.append(_from_slabs(conv_n).reshape(CONV_W - 1, b, dr).transpose(1, 0, 2))
            new_lru.append(_from_slabs(h_last))
        x = yield (x, ffn2, layer, 3 * layer + 2)
    return x.reshape(b, t, d), jnp.stack(new_ret), jnp.stack(new_conv), jnp.stack(new_lru)


def _run_trunks(trunks, ln):
    requests = [next(g) for g in trunks]
    results = [None] * len(trunks)
    while results[0] is None:
        _, w, layer, ln_idx = requests[0]
        outs = _ffn_ln([r[0] for r in requests], w, layer, ln, ln_idx)
        for i, g in enumerate(trunks):
            try:
                requests[i] = g.send(outs[i])
            except StopIteration as done:
                results[i] = done.value
    return results


def kernel(x_prompt, x_sample, state_ret, state_conv, state_lru, ln_g, ln_b, ffn1_w_in, ffn1_w_out,
           ffn2_w_in, ffn2_w_out, ret_w_in, ret_gn_g, ret_w_out, rec_w_in, rec_conv_w, rec_conv_b,
           rec_w_a, rec_b_a, rec_w_i, rec_b_i, rec_lam, rec_w_out):
    cast = lambda w: w.astype(BF16)
    d = ln_g.shape[-1]
    ln = (ln_g.reshape(-1, 1, d), ln_b.reshape(-1, 1, d))
    ffn1 = (cast(ffn1_w_in), cast(ffn1_w_out))
    ffn2 = (cast(ffn2_w_in), cast(ffn2_w_out))
    ret = (cast(ret_w_in), cast(ret_w_out), ret_gn_g.reshape(ret_gn_g.shape[0], 1, -1))
    rec = _prep_rec(rec_w_in, rec_conv_w, rec_conv_b, rec_w_a, rec_b_a, rec_w_i, rec_b_i, rec_lam,
                    rec_w_out)
    weights = (ln, ffn1, ffn2, ret, rec)
    tp, ts = x_prompt.shape[1], x_sample.shape[1]
    prompt = _trunk(x_prompt, jnp.arange(tp, dtype=jnp.int32), None, None, None, *weights)
    sample = _trunk(x_sample, PAST_LEN + jnp.arange(ts, dtype=jnp.int32),
                    state_ret, state_conv, state_lru, *weights)
    (y_p, ret_p, conv_p, lru_p), (y_s, ret_s, conv_s, lru_s) = _run_trunks([prompt, sample], ln)
    return (y_p, y_s, ret_p, conv_p, lru_p, ret_s, conv_s, lru_s)
```

```python
import functools
import math

import numpy as np
import jax
import jax.numpy as jnp
from jax import lax
from jax.experimental import pallas as pl
from jax.experimental.pallas import tpu as pltpu

F32 = jnp.float32
BF16 = jnp.bfloat16

DEPTH = 2
PAST_LEN = 16384
RET_HEADS = 4
RET_CHUNK = 128
RET_LONG_CHUNK = 256
ROPE_BASE = 10000.0
CONV_W = 4
LRU_C = 8.0
ALPHA = (2.0 * DEPTH) ** 0.25
LN_EPS = 1e-5
GN_EPS = 1e-6
RSQRT_FLOOR = 1e-30

V7X_SCOPED_VMEM_BYTES = 60000 * 1024
LANES = 128
SUBLANES = 8
ROW_TILE = 512
BF16_SUBLANES = 16
CAST_BLOCKS = 32
FF_CHUNK = 256
PROJ_CHUNK = 256
GATE_COLS = 256
GATE_ROWS = 512
SHORT_SEQ_BATCH_BLOCK = 4


def _params(semantics, vmem_bytes):
    return pltpu.CompilerParams(
        dimension_semantics=semantics,
        vmem_limit_bytes=int(min(V7X_SCOPED_VMEM_BYTES, vmem_bytes)),
    )


def _resident(shape):
    zeros = (0,) * len(shape)
    return pl.BlockSpec(shape, lambda *_: zeros, pipeline_mode=pl.Buffered(1))


def _resident_slice(shape, idx):
    index = (idx,) + (0,) * (len(shape) - 1)
    return pl.BlockSpec((None,) + tuple(shape[1:]), lambda *_: index, pipeline_mode=pl.Buffered(1))


def _nbytes(shape, dtype):
    return int(np.prod(shape)) * jnp.dtype(dtype).itemsize


def _layer_norm(y, g, b):
    mu = jnp.mean(y, axis=-1, keepdims=True)
    d = y - mu
    var = jnp.mean(d * d, axis=-1, keepdims=True)
    return d * lax.rsqrt(var + LN_EPS) * g + b


def _mm(a, b):
    return jnp.dot(a, b, preferred_element_type=F32)


def _cast_job(src, idx, steps):
    _, r, c = src.shape
    nb = CAST_BLOCKS
    while nb > steps or r % (nb * BF16_SUBLANES) != 0:
        nb //= 2
    blk = (None, r // nb, c)
    return (pl.BlockSpec(blk, lambda s: (idx, jnp.minimum(s, nb - 1), 0)),
            pl.BlockSpec(blk, lambda s: (0, jnp.minimum(s, nb - 1), 0)),
            jax.ShapeDtypeStruct((1, r, c), BF16),
            2 * (_nbytes(blk[1:], F32) + _nbytes(blk[1:], BF16)))


def _run_casts(src_refs, dst_refs):
    for src_ref, dst_ref in zip(src_refs, dst_refs):
        dst_ref[...] = src_ref[...].astype(BF16)


def _ffn_ln_kernel(*refs, ff, first_tile, n_cast):
    n = len(first_tile) - 1
    x_refs, (win_ref, wout_ref, g_ref, b_ref) = refs[:n], refs[n:n + 4]
    o_refs = refs[n + 4 + n_cast:2 * n + 4 + n_cast]
    _run_casts(refs[n + 4:n + 4 + n_cast], refs[2 * n + 4 + n_cast:])

    def body(x_ref, o_ref):
        x = x_ref[...]
        xb = x.astype(BF16)
        acc = jnp.zeros(x.shape, F32)
        for lo in range(0, ff, FF_CHUNK):
            gate = _mm(xb, win_ref[:, lo:lo + FF_CHUNK])
            up = _mm(xb, win_ref[:, ff + lo:ff + lo + FF_CHUNK])
            h = (gate * jax.nn.sigmoid(gate) * up).astype(BF16)
            acc = acc + _mm(h, wout_ref[lo:lo + FF_CHUNK, :])
        o_ref[...] = _layer_norm(ALPHA * x + 0.5 * acc, g_ref[...], b_ref[...])

    if n == 1:
        body(x_refs[0], o_refs[0])
    else:
        step = pl.program_id(0)
        for i in range(n):
            pl.when((step >= first_tile[i]) & (step < first_tile[i + 1]))(
                functools.partial(body, x_refs[i], o_refs[i]))


def _ffn_ln(xs, w, ln, ln_idx, casts=()):
    (w_in, i_in), (w_out, i_out) = w
    ln_g, ln_b = ln
    d = xs[0].shape[1]
    ff = w_out.shape[1]
    assert ff % FF_CHUNK == 0
    tms = [min(ROW_TILE, x.shape[0]) for x in xs]
    first_tile = [0]
    for x, tm in zip(xs, tms):
        first_tile.append(first_tile[-1] + x.shape[0] // tm)

    def rows(i):
        lo, n_tiles = first_tile[i], first_tile[i + 1] - first_tile[i]
        return pl.BlockSpec((tms[i], d), lambda s: (jnp.clip(s - lo, 0, n_tiles - 1), 0))

    specs = [rows(i) for i in range(len(xs))]
    jobs = [_cast_job(src, idx, first_tile[-1]) for src, idx in casts]
    vmem = (sum(4 * _nbytes((tm, d), F32) for tm in tms) + _nbytes(w_in.shape[1:], BF16)
            + _nbytes(w_out.shape[1:], BF16) + 8 * _nbytes((max(tms), d), F32)
            + sum(job[3] for job in jobs))
    outs = pl.pallas_call(
        functools.partial(_ffn_ln_kernel, ff=ff, first_tile=tuple(first_tile), n_cast=len(jobs)),
        grid=(first_tile[-1],),
        in_specs=specs + [_resident_slice(w_in.shape, i_in), _resident_slice(w_out.shape, i_out),
                          _resident_slice(ln_g.shape, ln_idx), _resident_slice(ln_b.shape, ln_idx)]
        + [job[0] for job in jobs],
        out_specs=specs + [job[1] for job in jobs],
        out_shape=[jax.ShapeDtypeStruct(x.shape, F32) for x in xs] + [job[2] for job in jobs],
        compiler_params=_params(("arbitrary",), vmem),
        name="ffn_ln",
    )(*xs, w_in, w_out, ln_g, ln_b, *[src for src, _ in casts])
    return outs[:len(xs)], outs[len(xs):]


def _proj_ln_kernel(a_ref, x_ref, w_ref, g_ref, b_ref, o_ref):
    y = ALPHA * x_ref[...] + _mm(a_ref[...], w_ref[...])
    o_ref[...] = _layer_norm(y, g_ref[...], b_ref[...])


def _proj_ln(a, x, w, j, ln, ln_idx):
    ln_g, ln_b = ln
    m, d = x.shape
    k = a.shape[1]
    tm = min(ROW_TILE, m)
    vmem = (2 * _nbytes((tm, k), BF16) + 4 * _nbytes((tm, d), F32) + _nbytes(w.shape[1:], BF16)
            + 4 * _nbytes((tm, d), F32))
    return pl.pallas_call(
        _proj_ln_kernel,
        grid=(m // tm,),
        in_specs=[pl.BlockSpec((tm, k), lambda i: (i, 0)), pl.BlockSpec((tm, d), lambda i: (i, 0)),
                  _resident_slice(w.shape, j),
                  _resident_slice(ln_g.shape, ln_idx), _resident_slice(ln_b.shape, ln_idx)],
        out_specs=pl.BlockSpec((tm, d), lambda i: (i, 0)),
        out_shape=jax.ShapeDtypeStruct((m, d), F32),
        compiler_params=_params(("parallel",), vmem),
        name="proj_ln",
    )(a, x, w, ln_g, ln_b)


def _ret_proj_kernel(x_ref, w_ref, cos_ref, sin_ref, *refs, heads, k_scale, n_cast):
    q_ref, k_ref, v_ref, g_ref = refs[n_cast:n_cast + 4]
    _run_casts(refs[:n_cast], refs[n_cast + 4:])
    xb = x_ref[...].astype(BF16)
    cos = cos_ref[...]
    sin = sin_ref[...]
    half = cos.shape[1]
    qk = q_ref.shape[1]
    vd = v_ref.shape[1]
    for col, dst, scale in ((0, q_ref, None), (qk, k_ref, k_scale)):
        p = _mm(xb, w_ref[:, col:col + qk])
        for h in range(heads):
            lo = 2 * half * h
            x1 = p[:, lo:lo + half]
            x2 = p[:, lo + half:lo + 2 * half]
            r1 = x1 * cos - x2 * sin
            r2 = x1 * sin + x2 * cos
            if scale is not None:
                r1 = r1 * scale
                r2 = r2 * scale
            dst[:, lo:lo + half] = r1.astype(BF16)
            dst[:, lo + half:lo + 2 * half] = r2.astype(BF16)
    v_ref[...] = _mm(xb, w_ref[:, 2 * qk:2 * qk + vd]).astype(BF16)
    g_ref[...] = _mm(xb, w_ref[:, 2 * qk + vd:]).astype(BF16)


def _ret_proj(x, w, cos, sin, qk, vd, casts=()):
    w, j = w
    m, d = x.shape
    tm = min(ROW_TILE, m)
    if cos.shape[0] < tm:
        reps = tm // cos.shape[0]
        cos, sin = jnp.tile(cos, (reps, 1)), jnp.tile(sin, (reps, 1))
    n_tab = cos.shape[0] // tm
    row = lambda n: pl.BlockSpec((tm, n), lambda i: (i, 0))
    tab = pl.BlockSpec((tm, cos.shape[1]), lambda i: (i % n_tab, 0))
    jobs = [_cast_job(src, idx, m // tm) for src, idx in casts]
    vmem = (2 * _nbytes((tm, d), F32) + _nbytes(w.shape[1:], BF16)
            + 4 * (_nbytes((tm, qk), BF16) + _nbytes((tm, vd), BF16))
            + 4 * _nbytes((tm, vd), F32) + sum(job[3] for job in jobs))
    dk = qk // RET_HEADS
    outs = pl.pallas_call(
        functools.partial(_ret_proj_kernel, heads=RET_HEADS, k_scale=dk ** -0.5, n_cast=len(jobs)),
        grid=(m // tm,),
        in_specs=[row(d), _resident_slice(w.shape, j), tab, tab] + [job[0] for job in jobs],
        out_specs=[row(qk), row(qk), row(vd), row(vd)] + [job[1] for job in jobs],
        out_shape=[jax.ShapeDtypeStruct((m, qk), BF16), jax.ShapeDtypeStruct((m, qk), BF16),
                   jax.ShapeDtypeStruct((m, vd), BF16), jax.ShapeDtypeStruct((m, vd), BF16)]
        + [job[2] for job in jobs],
        compiler_params=_params(("arbitrary",), vmem),
        name="ret_proj",
    )(x, w, cos, sin, *[src for src, _ in casts])
    return outs[:4], outs[4:]


def _retention_chunk(qc, kc, vc, s, dmask, q_dec, k_dec, chunk_dec):
    scores = lax.dot_general(qc, kc, (((1,), (1,)), ((), ())), preferred_element_type=F32) * dmask
    o = _mm(scores.astype(BF16), vc) + _mm((qc * q_dec).astype(BF16), s.astype(BF16))
    s_new = s * chunk_dec + lax.dot_general((kc * k_dec).astype(BF16), vc, (((0,), (0,)), ((), ())),
                                            preferred_element_type=F32)
    return o, s_new


def _norm_gate(o, gate, gn):
    mu = jnp.mean(o, axis=-1, keepdims=True)
    d = o - mu
    var = jnp.mean(d * d, axis=-1, keepdims=True)
    on = d * lax.rsqrt(var + GN_EPS) * gn
    gate = gate.astype(F32)
    return (gate * jax.nn.sigmoid(gate) * on).astype(BF16)


def _ret_core_kernel(q_ref, k_ref, v_ref, g_ref, gn_ref, dm_ref, qd_ref, kd_ref, cd_ref, s0_ref,
                     o_ref, s_ref, *, heads, chunk):
    bb, tt, _ = q_ref.shape
    dk = q_ref.shape[2] // heads
    dv = v_ref.shape[2] // heads

    @pl.when(pl.program_id(1) == 0)
    def _():
        s_ref[...] = s0_ref[...]

    for b in range(bb):
        for c in range(tt // chunk):
            for h in range(heads):
                rows = slice(c * chunk, (c + 1) * chunk)
                qk_cols = slice(h * dk, (h + 1) * dk)
                v_cols = slice(h * dv, (h + 1) * dv)
                o, s_ref[b, h] = _retention_chunk(
                    q_ref[b, rows, qk_cols], k_ref[b, rows, qk_cols], v_ref[b, rows, v_cols],
                    s_ref[b, h], dm_ref[h], qd_ref[h], kd_ref[h], cd_ref[h])
                o_ref[b, rows, v_cols] = _norm_gate(o, g_ref[b, rows, v_cols], gn_ref[:, v_cols])


def _ret_fused_kernel(q_ref, k_ref, v_ref, g_ref, x_ref, gn_ref, dm_ref, qd_ref, kd_ref, cd_ref,
                      wout_ref, lng_ref, lnb_ref, o_ref, s_ref, ret0_ref, ret1_ref, gated_ref,
                      *, heads, chunk, tiles_per_row, n_tiles):
    ret_ref = (ret0_ref, ret1_ref)
    tt = q_ref.shape[1]
    dk = q_ref.shape[2] // heads
    dv = v_ref.shape[2] // heads
    d_model = o_ref.shape[2]
    step = pl.program_id(0)
    chunks = [slice(c * chunk, (c + 1) * chunk) for c in range(tt // chunk)]
    v_cols = lambda h: slice(h * dv, (h + 1) * dv)

    @pl.when((step % tiles_per_row == 0) & (step < n_tiles))
    def _():
        s_ref[...] = jnp.zeros(s_ref.shape, F32)

    def attend(slot):
        for rows in chunks:
            for h in range(heads):
                qk_cols = slice(h * dk, (h + 1) * dk)
                ret_ref[slot][rows, v_cols(h)], s_ref[0, h] = _retention_chunk(
                    q_ref[0, rows, qk_cols], k_ref[0, rows, qk_cols], v_ref[0, rows, v_cols(h)],
                    s_ref[0, h], dm_ref[h], qd_ref[h], kd_ref[h], cd_ref[h])
                yield

    def finish(slot):
        for h in range(heads):
            for rows in chunks:
                gated_ref[rows, v_cols(h)] = _norm_gate(
                    ret_ref[slot][rows, v_cols(h)], g_ref[0, rows, v_cols(h)], gn_ref[:, v_cols(h)])
                yield
            for lo in range(0, d_model, PROJ_CHUNK):
                cols = slice(lo, lo + PROJ_CHUNK)
                part = _mm(gated_ref[:, v_cols(h)], wout_ref[v_cols(h), cols])
                o_ref[0, :, cols] = part if h == 0 else o_ref[0, :, cols] + part
                yield
        o_ref[0] = _layer_norm(ALPHA * x_ref[0] + o_ref[0], lng_ref[...], lnb_ref[...])
        yield

    @pl.when(step == 0)
    def _():
        _interleave(attend(0))

    for slot in range(2):
        @pl.when((step > 0) & (step < n_tiles) & (step % 2 == slot))
        def _(slot=slot):
            _interleave(attend(slot), finish(1 - slot))

    @pl.when(step == n_tiles)
    def _():
        _interleave(finish((n_tiles - 1) % 2))


def _decay_tables(chunk):
    lg = jnp.log1p(-jnp.exp2(-5.0 - jnp.arange(RET_HEADS, dtype=F32)))
    idx = jnp.arange(chunk, dtype=F32)
    rel = idx[:, None] - idx[None, :]
    dmask = jnp.where(rel >= 0, jnp.exp(lg[:, None, None] * jnp.maximum(rel, 0.0)), 0.0)
    q_dec = jnp.exp(lg[:, None] * (idx + 1.0))[:, :, None]
    k_dec = jnp.exp(lg[:, None] * (chunk - 1.0 - idx))[:, :, None]
    chunk_dec = jnp.exp(lg * chunk)[:, None, None]
    return dmask, q_dec, k_dec, chunk_dec


def _ret_core(q, k, v, g, gn_g, j, s0, *, bb, tt, chunk):
    b, t, qk = q.shape
    vd = v.shape[2]
    heads = RET_HEADS
    dk, dv = qk // heads, vd // heads
    tables = _decay_tables(chunk)
    blk = lambda n: pl.BlockSpec((bb, tt, n), lambda i, jj: (i, jj, 0))
    vmem = (4 * (_nbytes((bb, tt, qk), BF16) + _nbytes((bb, tt, vd), BF16))
            + 2 * _nbytes((bb, tt, vd), BF16) + 4 * _nbytes((bb, heads, dk, dv), F32)
            + 16 * _nbytes((max(chunk, SUBLANES), dv), F32) + 8 * _nbytes((dk, dv), F32))
    return pl.pallas_call(
        functools.partial(_ret_core_kernel, heads=heads, chunk=chunk),
        grid=(b // bb, t // tt),
        in_specs=[blk(qk), blk(qk), blk(vd), blk(vd), _resident_slice(gn_g.shape, j)]
        + [_resident(tab.shape) for tab in tables]
        + [pl.BlockSpec((None, bb, heads, dk, dv), lambda i, jj: (j, i, 0, 0, 0))],
        out_specs=[blk(vd), pl.BlockSpec((bb, heads, dk, dv), lambda i, jj: (i, 0, 0, 0))],
        out_shape=[jax.ShapeDtypeStruct((b, t, vd), BF16),
                   jax.ShapeDtypeStruct((b, heads, dk, dv), F32)],
        compiler_params=_params(("parallel", "arbitrary"), vmem),
        name="ret_core",
    )(q, k, v, g, gn_g, *tables, s0)


def _ret_fused(q, k, v, g, x, gn_g, w_out, j_out, j, ln, ln_idx, *, tt, chunk):
    ln_g, ln_b = ln
    b, t, qk = q.shape
    vd, d = v.shape[2], x.shape[2]
    heads = RET_HEADS
    dk, dv = qk // heads, vd // heads
    tables = _decay_tables(chunk)
    tiles_per_row = t // tt
    n_tiles = b * tiles_per_row
    assert tiles_per_row >= 2

    def tile_block(n, lag):
        def index(i):
            tile = jnp.clip(i - lag, 0, n_tiles - 1)
            return (tile // tiles_per_row, tile % tiles_per_row, 0)
        return pl.BlockSpec((1, tt, n), index)

    state = pl.BlockSpec((1, heads, dk, dv),
                         lambda i: (jnp.minimum(i, n_tiles - 1) // tiles_per_row, 0, 0, 0))
    vmem = (4 * (_nbytes((tt, qk), BF16) + _nbytes((tt, vd), BF16)) + 2 * _nbytes((tt, vd), BF16)
            + 4 * _nbytes((tt, d), F32) + 2 * _nbytes((heads, dk, dv), F32)
            + _nbytes(w_out.shape[1:], BF16) + 2 * _nbytes((tt, vd), F32) + _nbytes((tt, vd), BF16)
            + 16 * _nbytes((chunk, dv), F32) + 8 * _nbytes((dk, dv), F32) + 4 * _nbytes((tt, d), F32))
    return pl.pallas_call(
        functools.partial(_ret_fused_kernel, heads=heads, chunk=chunk, tiles_per_row=tiles_per_row,
                          n_tiles=n_tiles),
        grid=(n_tiles + 1,),
        in_specs=[tile_block(qk, 0), tile_block(qk, 0), tile_block(vd, 0), tile_block(vd, 1),
                  tile_block(d, 1), _resident_slice(gn_g.shape, j)]
        + [_resident(tab.shape) for tab in tables]
        + [_resident_slice(w_out.shape, j_out), _resident_slice(ln_g.shape, ln_idx),
           _resident_slice(ln_b.shape, ln_idx)],
        out_specs=[tile_block(d, 1), state],
        out_shape=[jax.ShapeDtypeStruct((b, t, d), F32),
                   jax.ShapeDtypeStruct((b, heads, dk, dv), F32)],
        scratch_shapes=[pltpu.VMEM((tt, vd), F32), pltpu.VMEM((tt, vd), F32),
                        pltpu.VMEM((tt, vd), BF16)],
        compiler_params=_params(("arbitrary",), vmem),
        name="ret_fused",
    )(q, k, v, g, x, gn_g, *tables, w_out, ln_g, ln_b)


def _gelu_tanh(x):
    c = math.sqrt(2.0 / math.pi)
    th = jnp.tanh(x * ((x * x) * (c * 0.044715) + c))
    hx = 0.5 * x
    return hx * th + hx


def _rglru_kernel(*refs, nb, pipelined, gate_windows):
    if pipelined:
        x_ref, xn_ref = refs[:2]
        refs = refs[2:]
    else:
        x_ref, xn_ref = refs[0], None
        refs = refs[1:]
    (win_ref, cw_ref, cb_ref, wbd_ref, ba_ref, bi_ref, lam_ref, conv0_ref, h0_ref, wout_ref,
     g_ref, b_ref, o_ref, convn_ref, hlast_ref,
     gate0_ref, gate1_ref, cin0_ref, cin1_ref, xpad_ref, a_ref, u_ref, h_ref) = refs
    gate_ref = (gate0_ref, gate1_ref)
    cin_ref = (cin0_ref, cin1_ref)
    batch_major = len(x_ref.shape) == 3
    dr = wout_ref.shape[0]
    n_slab = dr // LANES
    hist = (CONV_W - 1) * nb
    tm = a_ref.shape[1]
    nt = tm // nb
    lanes = lambda s: slice(s * LANES, (s + 1) * LANES)
    step = pl.program_id(0)

    def in_proj(src_ref, slot):
        xb = src_ref[...].reshape(tm, src_ref.shape[-1]).astype(BF16)
        for lo in range(0, dr, PROJ_CHUNK):
            cin_ref[slot][:, lo:lo + PROJ_CHUNK] = _mm(xb, win_ref[:, dr + lo:dr + lo + PROJ_CHUNK])
            yield
        for lo in range(0, dr, PROJ_CHUNK):
            gate_ref[slot][:, lo:lo + PROJ_CHUNK] = _gelu_tanh(_mm(xb, win_ref[:, lo:lo + PROJ_CHUNK]))
            yield

    @pl.when(step == 0)
    def _():
        xpad_ref[:, 0:hist, :] = conv0_ref[...]
        h_ref[...] = h0_ref[...]
        _interleave(in_proj(x_ref, 0))

    def recurrent(slot):
        for s in range(n_slab):
            if batch_major:
                for b in range(nb):
                    xpad_ref[s, pl.ds(hist + b, nt, stride=nb), :] = (
                        cin_ref[slot][b * nt:(b + 1) * nt, lanes(s)])
            else:
                xpad_ref[s, hist:hist + tm, :] = cin_ref[slot][:, lanes(s)]

        conv = None
        for j in range(CONV_W):
            term = cw_ref[j] * xpad_ref[:, j * nb:j * nb + tm, :]
            conv = term if conv is None else conv + term
        conv = cb_ref[...] + conv
        xc = jnp.concatenate([conv[s] for s in range(n_slab)], axis=1)

        tail = xpad_ref[:, tm:tm + hist, :]
        convn_ref[...] = tail
        xpad_ref[:, 0:hist, :] = tail
        yield

        lam = lam_ref[...]
        softplus_neg_lam = jnp.maximum(-lam, 0.0) + jnp.log1p(jnp.exp(-jnp.abs(lam)))
        c2 = (-0.5 * LRU_C * math.log2(math.e)) * softplus_neg_lam
        for jt, start in enumerate(gate_windows):
            cols = slice(jt * GATE_COLS, (jt + 1) * GATE_COLS)
            xw = xc[:, start:start + GATE_ROWS].astype(BF16)
            xh = xc[:, cols]
            th_a = jnp.tanh(_mm(xw, wbd_ref[0, jt]) + 0.5 * ba_ref[:, cols])
            th_i = jnp.tanh(_mm(xw, wbd_ref[1, jt]) + 0.5 * bi_ref[:, cols])
            a = jnp.exp2(c2[:, cols] * th_a + c2[:, cols])
            hx = 0.5 * xh
            y1 = 1.0 - a * a
            u = (y1 * lax.rsqrt(jnp.maximum(y1, RSQRT_FLOOR))) * (hx * th_i + hx)
            for s in range(GATE_COLS // LANES):
                a_ref[jt * (GATE_COLS // LANES) + s] = a[:, lanes(s)]
                u_ref[jt * (GATE_COLS // LANES) + s] = u[:, lanes(s)]
            yield

        h = h_ref[...]
        for t in range(nt):
            rows = slice(t * nb, (t + 1) * nb)
            h = a_ref[:, rows, :] * h + u_ref[:, rows, :]
            u_ref[:, rows, :] = h
        h_ref[...] = h
        hlast_ref[...] = h
        yield

        if batch_major:
            hs = jnp.concatenate(
                [jnp.concatenate([u_ref[s, pl.ds(b, nt, stride=nb), :] for b in range(nb)], axis=0)
                 for s in range(n_slab)], axis=1)
        else:
            hs = jnp.concatenate([u_ref[s] for s in range(n_slab)], axis=1)
        y = (gate_ref[slot][...] * hs).astype(BF16)
        x = x_ref[...].reshape(tm, x_ref.shape[-1])
        out = _layer_norm(ALPHA * x + _mm(y, wout_ref[...]), g_ref[...], b_ref[...])
        o_ref[...] = out.reshape(o_ref.shape)
        yield

    if pipelined:
        for slot in range(2):
            @pl.when(step % 2 == slot)
            def _(slot=slot):
                _interleave(recurrent(slot), in_proj(xn_ref, 1 - slot))
    else:
        _interleave(recurrent(0))


def _interleave(main, side=(), side_per_phase=2):
    side = iter(side)
    for _ in main:
        for _ in range(side_per_phase):
            next(side, None)
    for _ in side:
        pass


def _rglru(x, w_proj, w, j, conv0, h0, ln, ln_idx, *, nb):
    (w_in, j_in), (w_out, j_out) = w_proj
    cw, cb, wbd, ba, bi, lam, gate_windows = w
    ln_g, ln_b = ln
    dr, d = w_out.shape[1:]
    n_slab = dr // LANES
    hist = (CONV_W - 1) * nb
    if x.ndim == 3:
        nt = ROW_TILE // nb
        tm = ROW_TILE
        steps = x.shape[1] // nt
        block = (nb, nt, d)
        at = lambda i: (0, i, 0)
    else:
        tm = min(ROW_TILE, x.shape[0])
        steps = x.shape[0] // tm
        block = (tm, d)
        at = lambda i: (i, 0)
    assert tm % nb == 0 and tm >= hist and nb % SUBLANES == 0
    pipelined = steps > 1
    x_spec = pl.BlockSpec(block, at)
    x_specs, xs = [x_spec], [x]
    if pipelined:
        x_specs.append(pl.BlockSpec(block, lambda i: at(jnp.minimum(i + 1, steps - 1))))
        xs.append(x)
    vmem = ((2 + 2 * len(xs)) * _nbytes((tm, d), F32) + _nbytes(w_in.shape[1:], BF16)
            + _nbytes(wbd.shape[1:], BF16) + _nbytes(w_out.shape[1:], BF16)
            + 4 * _nbytes((hist + nb, dr), F32) + _nbytes((tm + hist, dr), F32)
            + 6 * _nbytes((tm, dr), F32) + _nbytes((nb, dr), F32) + 8 * _nbytes((tm, dr), F32))
    sl = _resident_slice
    return pl.pallas_call(
        functools.partial(_rglru_kernel, nb=nb, pipelined=pipelined, gate_windows=gate_windows),
        grid=(steps,),
        in_specs=x_specs + [sl(w_in.shape, j_in), sl(cw.shape, j), sl(cb.shape, j), sl(wbd.shape, j),
                            sl(ba.shape, j), sl(bi.shape, j), sl(lam.shape, j),
                            _resident(conv0.shape), _resident(h0.shape),
                            sl(w_out.shape, j_out), sl(ln_g.shape, ln_idx), sl(ln_b.shape, ln_idx)],
        out_specs=[x_spec,
                   pl.BlockSpec((n_slab, hist, LANES), lambda i: (0, 0, 0)),
                   pl.BlockSpec((n_slab, nb, LANES), lambda i: (0, 0, 0))],
        out_shape=[jax.ShapeDtypeStruct(x.shape, F32),
                   jax.ShapeDtypeStruct((n_slab, hist, LANES), F32),
                   jax.ShapeDtypeStruct((n_slab, nb, LANES), F32)],
        scratch_shapes=[pltpu.VMEM((tm, dr), F32), pltpu.VMEM((tm, dr), F32),
                        pltpu.VMEM((tm, dr), F32), pltpu.VMEM((tm, dr), F32),
                        pltpu.VMEM((n_slab, tm + hist, LANES), F32),
                        pltpu.VMEM((n_slab, tm, LANES), F32),
                        pltpu.VMEM((n_slab, tm, LANES), F32),
                        pltpu.VMEM((n_slab, nb, LANES), F32)],
        compiler_params=_params(("arbitrary",), vmem),
        name="rglru",
    )(*xs, w_in, cw, cb, wbd, ba, bi, lam, conv0, h0, w_out, ln_g, ln_b)


def _block_diag(blocks):
    n, bs, _ = blocks.shape
    cols = [jnp.pad(blocks[i], ((i * bs, (n - 1 - i) * bs), (0, 0))) for i in range(n)]
    return jnp.concatenate(cols, axis=1)


def _gate_windows(dr, block):
    starts = []
    for c0 in range(0, dr, GATE_COLS):
        lo = c0 // block * block
        hi = ((c0 + GATE_COLS - 1) // block + 1) * block
        start = min(lo // LANES * LANES, dr - GATE_ROWS)
        assert start <= lo and hi <= start + GATE_ROWS
        starts.append(start)
    return tuple(starts)


def _prep_rec(conv_w, conv_b, w_a, b_a, w_i, b_i, lam):
    n_layers, dr = conv_b.shape
    n_slab = dr // LANES
    windows = _gate_windows(dr, w_a.shape[2])
    wbd = jnp.stack([
        jnp.stack([
            jnp.stack([_block_diag(w[j])[start:start + GATE_ROWS, jt * GATE_COLS:(jt + 1) * GATE_COLS]
                       for jt, start in enumerate(windows)])
            for w in (w_a, w_i)])
        for j in range(n_layers)])
    wbd = (0.5 * wbd).astype(BF16)
    row = lambda v: v.reshape(n_layers, 1, dr)
    return (conv_w.reshape(n_layers, CONV_W, n_slab, 1, LANES),
            conv_b.reshape(n_layers, n_slab, 1, LANES), wbd, row(b_a), row(b_i), row(lam), windows)


def _to_slabs(a):
    rows, dr = a.shape
    return a.reshape(rows, dr // LANES, LANES).transpose(1, 0, 2)


def _from_slabs(a):
    n_slab, rows, _ = a.shape
    return a.transpose(1, 0, 2).reshape(rows, n_slab * LANES)


def _rope_tables(pos, dk):
    half = dk // 2
    inv = ROPE_BASE ** (-jnp.arange(half, dtype=F32) / half)
    ang = pos.astype(F32)[:, None] * inv[None, :]
    return jnp.cos(ang), jnp.sin(ang)


def _to_time_major(a, b, t):
    return a.reshape(b, t, a.shape[-1]).transpose(1, 0, 2).reshape(t * b, a.shape[-1])


def _to_batch_major(a, b, t):
    return a.reshape(t, b, a.shape[-1]).transpose(1, 0, 2).reshape(b * t, a.shape[-1])


def _trunk(x, pos, state_ret, state_conv, state_lru, ln, bank, gn_g, rec):
    b, t, d = x.shape
    m = b * t
    x = x.reshape(m, d)
    long_seq = t >= ROW_TILE
    new_ret, new_conv, new_lru = [], [], []
    for layer in range(DEPTH):
        j = layer // 2
        x = yield (x, "ffn1", layer, 3 * layer)
        if layer % 2 == 0:
            w_out, j_out = bank.get(f"ret_out_{j}")
            vd, qk = w_out.shape[1:]
            cos, sin = _rope_tables(pos, qk // RET_HEADS)
            chunk = math.gcd(t, RET_CHUNK)
            pending = bank.pending(_cast_plan()["ret_proj", j])
            (q, k, v, g), converted = _ret_proj(x, bank.get(f"ret_in_{j}"), cos, sin, qk, vd,
                                                casts=[src for _, src in pending])
            bank.put([name for name, _ in pending], converted)
            qkvg = (q.reshape(b, t, qk), k.reshape(b, t, qk), v.reshape(b, t, vd), g.reshape(b, t, vd))
            if state_ret is None:
                assert long_seq
                x, s_new = _ret_fused(*qkvg, x.reshape(b, t, d), gn_g, w_out, j_out, j, ln,
                                      3 * layer + 1, tt=ROW_TILE,
                                      chunk=math.gcd(ROW_TILE, RET_LONG_CHUNK))
                x = x.reshape(m, d)
            else:
                assert not long_seq
                gated, s_new = _ret_core(*qkvg, gn_g, j, state_ret,
                                         bb=SHORT_SEQ_BATCH_BLOCK, tt=t, chunk=chunk)
                x = _proj_ln(gated.reshape(m, vd), x, w_out, j_out, ln, 3 * layer + 1)
            new_ret.append(s_new)
        else:
            w_proj = (bank.get(f"rec_in_{j}"), bank.get(f"rec_out_{j}"))
            dr = w_proj[1][0].shape[1]
            if state_conv is None:
                conv0 = jnp.zeros((dr // LANES, (CONV_W - 1) * b, LANES), F32)
                h0 = jnp.zeros((dr // LANES, b, LANES), F32)
            else:
                conv0 = _to_slabs(state_conv[j].transpose(1, 0, 2).reshape((CONV_W - 1) * b, dr))
                h0 = _to_slabs(state_lru[j])
            x_in = x.reshape(b, t, d) if long_seq else _to_time_major(x, b, t)
            x_out, conv_n, h_last = _rglru(x_in, w_proj, rec, j, conv0, h0, ln, 3 * layer + 1, nb=b)
            x = x_out.reshape(m, d) if long_seq else _to_batch_major(x_out, b, t)
            new_conv.append(_from_slabs(conv_n).reshape(CONV_W - 1, b, dr).transpose(1, 0, 2))
            new_lru.append(_from_slabs(h_last))
        x = yield (x, "ffn2", layer, 3 * layer + 2)
    return x.reshape(b, t, d), jnp.stack(new_ret), jnp.stack(new_conv), jnp.stack(new_lru)


class _WeightBank:
    def __init__(self, f32):
        self.f32 = f32
        self.bf16 = {}

    def get(self, name):
        return self.bf16[name]

    def pending(self, names):
        return [(name, self.f32[name]) for name in names if name not in self.bf16]

    def put(self, names, arrays):
        for name, array in zip(names, arrays):
            self.bf16[name] = (array, 0)


def _cast_plan():
    plan = {}
    for layer in range(DEPTH):
        j = layer // 2
        ffn2 = [f"ffn2_in_{layer}", f"ffn2_out_{layer}"]
        if layer % 2 == 0:
            plan["ffn1", layer] = [f"ret_in_{j}", f"ret_out_{j}"]
            plan["ret_proj", j] = ffn2
        else:
            plan["ffn1", layer] = [f"rec_in_{j}", f"rec_out_{j}"] + ffn2
        plan["ffn2", layer] = ([f"ffn1_in_{layer + 1}", f"ffn1_out_{layer + 1}"]
                               if layer + 1 < DEPTH else [])
    return plan


def _run_trunks(trunks, ln, bank):
    requests = [next(g) for g in trunks]
    results = [None] * len(trunks)
    while results[0] is None:
        _, which, layer, ln_idx = requests[0]
        pending = bank.pending(_cast_plan()[which, layer])
        outs, converted = _ffn_ln([r[0] for r in requests],
                                  (bank.get(f"{which}_in_{layer}"), bank.get(f"{which}_out_{layer}")),
                                  ln, ln_idx, casts=[src for _, src in pending])
        bank.put([name for name, _ in pending], converted)
        for i, g in enumerate(trunks):
            try:
                requests[i] = g.send(outs[i])
            except StopIteration as done:
                results[i] = done.value
    return results


def kernel(x_prompt, x_sample, state_ret, state_conv, state_lru, ln_g, ln_b, ffn1_w_in, ffn1_w_out,
           ffn2_w_in, ffn2_w_out, ret_w_in, ret_gn_g, ret_w_out, rec_w_in, rec_conv_w, rec_conv_b,
           rec_w_a, rec_b_a, rec_w_i, rec_b_i, rec_lam, rec_w_out):
    d = ln_g.shape[-1]
    ln = (ln_g.reshape(-1, 1, d), ln_b.reshape(-1, 1, d))
    stacked = {"ffn1_in": ffn1_w_in, "ffn1_out": ffn1_w_out, "ffn2_in": ffn2_w_in,
               "ffn2_out": ffn2_w_out, "ret_in": ret_w_in, "ret_out": ret_w_out,
               "rec_in": rec_w_in, "rec_out": rec_w_out}
    bank = _WeightBank({f"{name}_{i}": (w, i) for name, w in stacked.items()
                        for i in range(w.shape[0])})
    first = ["ffn1_in_0", "ffn1_out_0"]
    bank.put(first, [bank.f32[name][0][0:1].astype(BF16) for name in first])
    gn_g = ret_gn_g.reshape(ret_gn_g.shape[0], 1, -1)
    rec = _prep_rec(rec_conv_w, rec_conv_b, rec_w_a, rec_b_a, rec_w_i, rec_b_i, rec_lam)
    tp, ts = x_prompt.shape[1], x_sample.shape[1]
    prompt = _trunk(x_prompt, jnp.arange(tp, dtype=jnp.int32), None, None, None, ln, bank, gn_g, rec)
    sample = _trunk(x_sample, PAST_LEN + jnp.arange(ts, dtype=jnp.int32),
                    state_ret, state_conv, state_lru, ln, bank, gn_g, rec)
    (y_p, ret_p, conv_p, lru_p), (y_s, ret_s, conv_s, lru_s) = _run_trunks([prompt, sample], ln, bank)
    return (y_p, y_s, ret_p, conv_p, lru_p, ret_s, conv_s, lru_s)
```

```python
import functools
import math

import numpy as np
import jax
import jax.numpy as jnp
from jax import lax
from jax.experimental import pallas as pl
from jax.experimental.pallas import tpu as pltpu

F32 = jnp.float32
BF16 = jnp.bfloat16

DEPTH = 2
PAST_LEN = 16384
RET_HEADS = 4
RET_CHUNK = 128
RET_LONG_CHUNK = 256
ROPE_BASE = 10000.0
CONV_W = 4
LRU_C = 8.0
ALPHA = (2.0 * DEPTH) ** 0.25
LN_EPS = 1e-5
GN_EPS = 1e-6
RSQRT_FLOOR = 1e-30

V7X_SCOPED_VMEM_BYTES = 60000 * 1024
LANES = 128
SUBLANES = 8
ROW_TILE = 512
BF16_SUBLANES = 16
CAST_BLOCKS = 32
FF_CHUNK = 256
PROJ_CHUNK = 256
RGLRU_SIDE_STEPS = (1, 1, 1, 1, 1, 1, 2, 2)
GATE_COLS = 256
GATE_ROWS = 512
SHORT_SEQ_BATCH_BLOCK = 4


def _params(semantics, vmem_bytes):
    return pltpu.CompilerParams(
        dimension_semantics=semantics,
        vmem_limit_bytes=int(min(V7X_SCOPED_VMEM_BYTES, vmem_bytes)),
    )


def _resident(shape):
    zeros = (0,) * len(shape)
    return pl.BlockSpec(shape, lambda *_: zeros, pipeline_mode=pl.Buffered(1))


def _resident_slice(shape, idx):
    index = (idx,) + (0,) * (len(shape) - 1)
    return pl.BlockSpec((None,) + tuple(shape[1:]), lambda *_: index, pipeline_mode=pl.Buffered(1))


def _nbytes(shape, dtype):
    return int(np.prod(shape)) * jnp.dtype(dtype).itemsize


def _layer_norm(y, g, b):
    mu = jnp.mean(y, axis=-1, keepdims=True)
    d = y - mu
    var = jnp.mean(d * d, axis=-1, keepdims=True)
    return d * lax.rsqrt(var + LN_EPS) * g + b


def _mm(a, b):
    return jnp.dot(a, b, preferred_element_type=F32)


def _cast_job(src, idx, steps):
    _, r, c = src.shape
    nb = CAST_BLOCKS
    while nb > steps or r % (nb * BF16_SUBLANES) != 0:
        nb //= 2
    blk = (None, r // nb, c)
    return (pl.BlockSpec(blk, lambda s: (idx, jnp.minimum(s, nb - 1), 0)),
            pl.BlockSpec(blk, lambda s: (0, jnp.minimum(s, nb - 1), 0)),
            jax.ShapeDtypeStruct((1, r, c), BF16),
            2 * (_nbytes(blk[1:], F32) + _nbytes(blk[1:], BF16)))


def _run_casts(src_refs, dst_refs):
    for src_ref, dst_ref in zip(src_refs, dst_refs):
        dst_ref[...] = src_ref[...].astype(BF16)


def _convert_kernel(*refs):
    _run_casts(refs[:len(refs) // 2], refs[len(refs) // 2:])


def _convert(casts):
    jobs = [_cast_job(src, idx, CAST_BLOCKS) for src, idx in casts]
    return pl.pallas_call(
        _convert_kernel,
        grid=(CAST_BLOCKS,),
        in_specs=[job[0] for job in jobs],
        out_specs=[job[1] for job in jobs],
        out_shape=[job[2] for job in jobs],
        compiler_params=_params(("arbitrary",), 2 * sum(job[3] for job in jobs)),
        name="convert",
    )(*[src for src, _ in casts])


def _ffn_ln_kernel(*refs, ff, first_tile, n_cast):
    n = len(first_tile) - 1
    x_refs, (win_ref, wout_ref, g_ref, b_ref) = refs[:n], refs[n:n + 4]
    o_refs = refs[n + 4 + n_cast:2 * n + 4 + n_cast]
    _run_casts(refs[n + 4:n + 4 + n_cast], refs[2 * n + 4 + n_cast:])

    def body(x_ref, o_ref):
        x = x_ref[...]
        xb = x.astype(BF16)
        acc = jnp.zeros(x.shape, F32)
        for lo in range(0, ff, FF_CHUNK):
            gate = _mm(xb, win_ref[:, lo:lo + FF_CHUNK])
            up = _mm(xb, win_ref[:, ff + lo:ff + lo + FF_CHUNK])
            h = (gate * jax.nn.sigmoid(gate) * up).astype(BF16)
            acc = acc + _mm(h, wout_ref[lo:lo + FF_CHUNK, :])
        o_ref[...] = _layer_norm(ALPHA * x + 0.5 * acc, g_ref[...], b_ref[...])

    if n == 1:
        body(x_refs[0], o_refs[0])
    else:
        step = pl.program_id(0)
        for i in range(n):
            pl.when((step >= first_tile[i]) & (step < first_tile[i + 1]))(
                functools.partial(body, x_refs[i], o_refs[i]))


def _ffn_ln(xs, w, ln, ln_idx, casts=()):
    (w_in, i_in), (w_out, i_out) = w
    ln_g, ln_b = ln
    d = xs[0].shape[1]
    ff = w_out.shape[1]
    assert ff % FF_CHUNK == 0
    tms = [min(ROW_TILE, x.shape[0]) for x in xs]
    first_tile = [0]
    for x, tm in zip(xs, tms):
        first_tile.append(first_tile[-1] + x.shape[0] // tm)

    def rows(i):
        lo, n_tiles = first_tile[i], first_tile[i + 1] - first_tile[i]
        return pl.BlockSpec((tms[i], d), lambda s: (jnp.clip(s - lo, 0, n_tiles - 1), 0))

    specs = [rows(i) for i in range(len(xs))]
    jobs = [_cast_job(src, idx, first_tile[-1]) for src, idx in casts]
    vmem = (sum(4 * _nbytes((tm, d), F32) for tm in tms) + _nbytes(w_in.shape[1:], BF16)
            + _nbytes(w_out.shape[1:], BF16) + 8 * _nbytes((max(tms), d), F32)
            + sum(job[3] for job in jobs))
    outs = pl.pallas_call(
        functools.partial(_ffn_ln_kernel, ff=ff, first_tile=tuple(first_tile), n_cast=len(jobs)),
        grid=(first_tile[-1],),
        in_specs=specs + [_resident_slice(w_in.shape, i_in), _resident_slice(w_out.shape, i_out),
                          _resident_slice(ln_g.shape, ln_idx), _resident_slice(ln_b.shape, ln_idx)]
        + [job[0] for job in jobs],
        out_specs=specs + [job[1] for job in jobs],
        out_shape=[jax.ShapeDtypeStruct(x.shape, F32) for x in xs] + [job[2] for job in jobs],
        compiler_params=_params(("arbitrary",), vmem),
        name="ffn_ln",
    )(*xs, w_in, w_out, ln_g, ln_b, *[src for src, _ in casts])
    return outs[:len(xs)], outs[len(xs):]


def _proj_ln_kernel(a_ref, x_ref, w_ref, g_ref, b_ref, o_ref):
    y = ALPHA * x_ref[...] + _mm(a_ref[...], w_ref[...])
    o_ref[...] = _layer_norm(y, g_ref[...], b_ref[...])


def _proj_ln(a, x, w, j, ln, ln_idx):
    ln_g, ln_b = ln
    m, d = x.shape
    k = a.shape[1]
    tm = min(ROW_TILE, m)
    vmem = (2 * _nbytes((tm, k), BF16) + 4 * _nbytes((tm, d), F32) + _nbytes(w.shape[1:], BF16)
            + 4 * _nbytes((tm, d), F32))
    return pl.pallas_call(
        _proj_ln_kernel,
        grid=(m // tm,),
        in_specs=[pl.BlockSpec((tm, k), lambda i: (i, 0)), pl.BlockSpec((tm, d), lambda i: (i, 0)),
                  _resident_slice(w.shape, j),
                  _resident_slice(ln_g.shape, ln_idx), _resident_slice(ln_b.shape, ln_idx)],
        out_specs=pl.BlockSpec((tm, d), lambda i: (i, 0)),
        out_shape=jax.ShapeDtypeStruct((m, d), F32),
        compiler_params=_params(("parallel",), vmem),
        name="proj_ln",
    )(a, x, w, ln_g, ln_b)


def _ret_proj_kernel(x_ref, w_ref, cos_ref, sin_ref, *refs, heads, k_scale, n_cast):
    q_ref, k_ref, v_ref, g_ref = refs[n_cast:n_cast + 4]
    _run_casts(refs[:n_cast], refs[n_cast + 4:])
    xb = x_ref[...].astype(BF16)
    cos = cos_ref[...]
    sin = sin_ref[...]
    half = cos.shape[1]
    qk = q_ref.shape[1]
    vd = v_ref.shape[1]
    for col, dst, scale in ((0, q_ref, None), (qk, k_ref, k_scale)):
        p = _mm(xb, w_ref[:, col:col + qk])
        for h in range(heads):
            lo = 2 * half * h
            x1 = p[:, lo:lo + half]
            x2 = p[:, lo + half:lo + 2 * half]
            r1 = x1 * cos - x2 * sin
            r2 = x1 * sin + x2 * cos
            if scale is not None:
                r1 = r1 * scale
                r2 = r2 * scale
            dst[:, lo:lo + half] = r1.astype(BF16)
            dst[:, lo + half:lo + 2 * half] = r2.astype(BF16)
    v_ref[...] = _mm(xb, w_ref[:, 2 * qk:2 * qk + vd]).astype(BF16)
    g_ref[...] = _mm(xb, w_ref[:, 2 * qk + vd:]).astype(BF16)


def _ret_proj(x, w, cos, sin, qk, vd, casts=()):
    w, j = w
    m, d = x.shape
    tm = min(ROW_TILE, m)
    if cos.shape[0] < tm:
        reps = tm // cos.shape[0]
        cos, sin = jnp.tile(cos, (reps, 1)), jnp.tile(sin, (reps, 1))
    n_tab = cos.shape[0] // tm
    row = lambda n: pl.BlockSpec((tm, n), lambda i: (i, 0))
    tab = pl.BlockSpec((tm, cos.shape[1]), lambda i: (i % n_tab, 0))
    jobs = [_cast_job(src, idx, m // tm) for src, idx in casts]
    vmem = (2 * _nbytes((tm, d), F32) + _nbytes(w.shape[1:], BF16)
            + 4 * (_nbytes((tm, qk), BF16) + _nbytes((tm, vd), BF16))
            + 4 * _nbytes((tm, vd), F32) + sum(job[3] for job in jobs))
    dk = qk // RET_HEADS
    outs = pl.pallas_call(
        functools.partial(_ret_proj_kernel, heads=RET_HEADS, k_scale=dk ** -0.5, n_cast=len(jobs)),
        grid=(m // tm,),
        in_specs=[row(d), _resident_slice(w.shape, j), tab, tab] + [job[0] for job in jobs],
        out_specs=[row(qk), row(qk), row(vd), row(vd)] + [job[1] for job in jobs],
        out_shape=[jax.ShapeDtypeStruct((m, qk), BF16), jax.ShapeDtypeStruct((m, qk), BF16),
                   jax.ShapeDtypeStruct((m, vd), BF16), jax.ShapeDtypeStruct((m, vd), BF16)]
        + [job[2] for job in jobs],
        compiler_params=_params(("arbitrary",), vmem),
        name="ret_proj",
    )(x, w, cos, sin, *[src for src, _ in casts])
    return outs[:4], outs[4:]


def _retention_chunk(qc, kc, vc, s, dmask, q_dec, k_dec, chunk_dec):
    scores = lax.dot_general(qc, kc, (((1,), (1,)), ((), ())), preferred_element_type=F32) * dmask
    o = _mm(scores.astype(BF16), vc) + _mm((qc * q_dec).astype(BF16), s.astype(BF16))
    s_new = s * chunk_dec + lax.dot_general((kc * k_dec).astype(BF16), vc, (((0,), (0,)), ((), ())),
                                            preferred_element_type=F32)
    return o, s_new


def _norm_gate(o, gate, gn):
    mu = jnp.mean(o, axis=-1, keepdims=True)
    d = o - mu
    var = jnp.mean(d * d, axis=-1, keepdims=True)
    on = d * lax.rsqrt(var + GN_EPS) * gn
    gate = gate.astype(F32)
    return (gate * jax.nn.sigmoid(gate) * on).astype(BF16)


def _ret_core_kernel(q_ref, k_ref, v_ref, g_ref, gn_ref, dm_ref, qd_ref, kd_ref, cd_ref, s0_ref,
                     o_ref, s_ref, *, heads, chunk):
    bb, tt, _ = q_ref.shape
    dk = q_ref.shape[2] // heads
    dv = v_ref.shape[2] // heads

    @pl.when(pl.program_id(1) == 0)
    def _():
        s_ref[...] = s0_ref[...]

    for b in range(bb):
        for c in range(tt // chunk):
            for h in range(heads):
                rows = slice(c * chunk, (c + 1) * chunk)
                qk_cols = slice(h * dk, (h + 1) * dk)
                v_cols = slice(h * dv, (h + 1) * dv)
                o, s_ref[b, h] = _retention_chunk(
                    q_ref[b, rows, qk_cols], k_ref[b, rows, qk_cols], v_ref[b, rows, v_cols],
                    s_ref[b, h], dm_ref[h], qd_ref[h], kd_ref[h], cd_ref[h])
                o_ref[b, rows, v_cols] = _norm_gate(o, g_ref[b, rows, v_cols], gn_ref[:, v_cols])


def _ret_fused_kernel(q_ref, k_ref, v_ref, g_ref, x_ref, gn_ref, dm_ref, qd_ref, kd_ref, cd_ref,
                      wout_ref, lng_ref, lnb_ref, o_ref, s_ref, ret0_ref, ret1_ref, gated_ref,
                      *, heads, chunk, tiles_per_row, n_tiles):
    ret_ref = (ret0_ref, ret1_ref)
    tt = q_ref.shape[1]
    dk = q_ref.shape[2] // heads
    dv = v_ref.shape[2] // heads
    d_model = o_ref.shape[2]
    step = pl.program_id(0)
    chunks = [slice(c * chunk, (c + 1) * chunk) for c in range(tt // chunk)]
    v_cols = lambda h: slice(h * dv, (h + 1) * dv)

    @pl.when((step % tiles_per_row == 0) & (step < n_tiles))
    def _():
        s_ref[...] = jnp.zeros(s_ref.shape, F32)

    def attend(slot):
        for rows in chunks:
            for h in range(heads):
                qk_cols = slice(h * dk, (h + 1) * dk)
                ret_ref[slot][rows, v_cols(h)], s_ref[0, h] = _retention_chunk(
                    q_ref[0, rows, qk_cols], k_ref[0, rows, qk_cols], v_ref[0, rows, v_cols(h)],
                    s_ref[0, h], dm_ref[h], qd_ref[h], kd_ref[h], cd_ref[h])
                yield

    def finish(slot):
        for h in range(heads):
            for rows in chunks:
                gated_ref[rows, v_cols(h)] = _norm_gate(
                    ret_ref[slot][rows, v_cols(h)], g_ref[0, rows, v_cols(h)], gn_ref[:, v_cols(h)])
                yield
            for lo in range(0, d_model, PROJ_CHUNK):
                cols = slice(lo, lo + PROJ_CHUNK)
                part = _mm(gated_ref[:, v_cols(h)], wout_ref[v_cols(h), cols])
                o_ref[0, :, cols] = part if h == 0 else o_ref[0, :, cols] + part
                yield
        o_ref[0] = _layer_norm(ALPHA * x_ref[0] + o_ref[0], lng_ref[...], lnb_ref[...])
        yield

    @pl.when(step == 0)
    def _():
        _interleave(attend(0))

    for slot in range(2):
        @pl.when((step > 0) & (step < n_tiles) & (step % 2 == slot))
        def _(slot=slot):
            _interleave(attend(slot), finish(1 - slot))

    @pl.when(step == n_tiles)
    def _():
        _interleave(finish((n_tiles - 1) % 2))


def _decay_tables(chunk):
    lg = jnp.log1p(-jnp.exp2(-5.0 - jnp.arange(RET_HEADS, dtype=F32)))
    idx = jnp.arange(chunk, dtype=F32)
    rel = idx[:, None] - idx[None, :]
    dmask = jnp.where(rel >= 0, jnp.exp(lg[:, None, None] * jnp.maximum(rel, 0.0)), 0.0)
    q_dec = jnp.exp(lg[:, None] * (idx + 1.0))[:, :, None]
    k_dec = jnp.exp(lg[:, None] * (chunk - 1.0 - idx))[:, :, None]
    chunk_dec = jnp.exp(lg * chunk)[:, None, None]
    return dmask, q_dec, k_dec, chunk_dec


def _ret_core(q, k, v, g, gn_g, j, s0, *, bb, tt, chunk):
    b, t, qk = q.shape
    vd = v.shape[2]
    heads = RET_HEADS
    dk, dv = qk // heads, vd // heads
    tables = _decay_tables(chunk)
    blk = lambda n: pl.BlockSpec((bb, tt, n), lambda i, jj: (i, jj, 0))
    vmem = (4 * (_nbytes((bb, tt, qk), BF16) + _nbytes((bb, tt, vd), BF16))
            + 2 * _nbytes((bb, tt, vd), BF16) + 4 * _nbytes((bb, heads, dk, dv), F32)
            + 16 * _nbytes((max(chunk, SUBLANES), dv), F32) + 8 * _nbytes((dk, dv), F32))
    return pl.pallas_call(
        functools.partial(_ret_core_kernel, heads=heads, chunk=chunk),
        grid=(b // bb, t // tt),
        in_specs=[blk(qk), blk(qk), blk(vd), blk(vd), _resident_slice(gn_g.shape, j)]
        + [_resident(tab.shape) for tab in tables]
        + [pl.BlockSpec((None, bb, heads, dk, dv), lambda i, jj: (j, i, 0, 0, 0))],
        out_specs=[blk(vd), pl.BlockSpec((bb, heads, dk, dv), lambda i, jj: (i, 0, 0, 0))],
        out_shape=[jax.ShapeDtypeStruct((b, t, vd), BF16),
                   jax.ShapeDtypeStruct((b, heads, dk, dv), F32)],
        compiler_params=_params(("parallel", "arbitrary"), vmem),
        name="ret_core",
    )(q, k, v, g, gn_g, *tables, s0)


def _ret_fused(q, k, v, g, x, gn_g, w_out, j_out, j, ln, ln_idx, *, tt, chunk):
    ln_g, ln_b = ln
    b, t, qk = q.shape
    vd, d = v.shape[2], x.shape[2]
    heads = RET_HEADS
    dk, dv = qk // heads, vd // heads
    tables = _decay_tables(chunk)
    tiles_per_row = t // tt
    n_tiles = b * tiles_per_row
    assert tiles_per_row >= 2

    def tile_block(n, lag):
        def index(i):
            tile = jnp.clip(i - lag, 0, n_tiles - 1)
            return (tile // tiles_per_row, tile % tiles_per_row, 0)
        return pl.BlockSpec((1, tt, n), index)

    state = pl.BlockSpec((1, heads, dk, dv),
                         lambda i: (jnp.minimum(i, n_tiles - 1) // tiles_per_row, 0, 0, 0))
    vmem = (4 * (_nbytes((tt, qk), BF16) + _nbytes((tt, vd), BF16)) + 2 * _nbytes((tt, vd), BF16)
            + 4 * _nbytes((tt, d), F32) + 2 * _nbytes((heads, dk, dv), F32)
            + _nbytes(w_out.shape[1:], BF16) + 2 * _nbytes((tt, vd), F32) + _nbytes((tt, vd), BF16)
            + 16 * _nbytes((chunk, dv), F32) + 8 * _nbytes((dk, dv), F32) + 4 * _nbytes((tt, d), F32))
    return pl.pallas_call(
        functools.partial(_ret_fused_kernel, heads=heads, chunk=chunk, tiles_per_row=tiles_per_row,
                          n_tiles=n_tiles),
        grid=(n_tiles + 1,),
        in_specs=[tile_block(qk, 0), tile_block(qk, 0), tile_block(vd, 0), tile_block(vd, 1),
                  tile_block(d, 1), _resident_slice(gn_g.shape, j)]
        + [_resident(tab.shape) for tab in tables]
        + [_resident_slice(w_out.shape, j_out), _resident_slice(ln_g.shape, ln_idx),
           _resident_slice(ln_b.shape, ln_idx)],
        out_specs=[tile_block(d, 1), state],
        out_shape=[jax.ShapeDtypeStruct((b, t, d), F32),
                   jax.ShapeDtypeStruct((b, heads, dk, dv), F32)],
        scratch_shapes=[pltpu.VMEM((tt, vd), F32), pltpu.VMEM((tt, vd), F32),
                        pltpu.VMEM((tt, vd), BF16)],
        compiler_params=_params(("arbitrary",), vmem),
        name="ret_fused",
    )(q, k, v, g, x, gn_g, *tables, w_out, ln_g, ln_b)


def _gelu_tanh(x):
    c = math.sqrt(2.0 / math.pi)
    th = jnp.tanh(x * ((x * x) * (c * 0.044715) + c))
    hx = 0.5 * x
    return hx * th + hx


def _rglru_kernel(*refs, nb, pipelined, gate_windows):
    if pipelined:
        x_ref, xn_ref = refs[:2]
        refs = refs[2:]
    else:
        x_ref, xn_ref = refs[0], None
        refs = refs[1:]
    (win_ref, cw_ref, cb_ref, wbd_ref, ba_ref, bi_ref, lam_ref, conv0_ref, h0_ref, wout_ref,
     g_ref, b_ref, o_ref, convn_ref, hlast_ref,
     gate0_ref, gate1_ref, cin0_ref, cin1_ref, xpad_ref, a_ref, u_ref, h_ref) = refs
    gate_ref = (gate0_ref, gate1_ref)
    cin_ref = (cin0_ref, cin1_ref)
    batch_major = len(x_ref.shape) == 3
    dr = wout_ref.shape[0]
    n_slab = dr // LANES
    hist = (CONV_W - 1) * nb
    tm = a_ref.shape[1]
    nt = tm // nb
    lanes = lambda s: slice(s * LANES, (s + 1) * LANES)
    step = pl.program_id(0)

    def in_proj(src_ref, slot):
        xb = src_ref[...].reshape(tm, src_ref.shape[-1]).astype(BF16)
        for lo in range(0, dr, PROJ_CHUNK):
            cin_ref[slot][:, lo:lo + PROJ_CHUNK] = _mm(xb, win_ref[:, dr + lo:dr + lo + PROJ_CHUNK])
            yield
        for lo in range(0, dr, PROJ_CHUNK):
            gate_ref[slot][:, lo:lo + PROJ_CHUNK] = _gelu_tanh(_mm(xb, win_ref[:, lo:lo + PROJ_CHUNK]))
            yield

    @pl.when(step == 0)
    def _():
        xpad_ref[:, 0:hist, :] = conv0_ref[...]
        h_ref[...] = h0_ref[...]
        _interleave(in_proj(x_ref, 0))

    def recurrent(slot):
        for s in range(n_slab):
            if batch_major:
                for b in range(nb):
                    xpad_ref[s, pl.ds(hist + b, nt, stride=nb), :] = (
                        cin_ref[slot][b * nt:(b + 1) * nt, lanes(s)])
            else:
                xpad_ref[s, hist:hist + tm, :] = cin_ref[slot][:, lanes(s)]

        conv = None
        for j in range(CONV_W):
            term = cw_ref[j] * xpad_ref[:, j * nb:j * nb + tm, :]
            conv = term if conv is None else conv + term
        conv = cb_ref[...] + conv
        xc = jnp.concatenate([conv[s] for s in range(n_slab)], axis=1)

        tail = xpad_ref[:, tm:tm + hist, :]
        convn_ref[...] = tail
        xpad_ref[:, 0:hist, :] = tail
        yield

        lam = lam_ref[...]
        softplus_neg_lam = jnp.maximum(-lam, 0.0) + jnp.log1p(jnp.exp(-jnp.abs(lam)))
        c2 = (-0.5 * LRU_C * math.log2(math.e)) * softplus_neg_lam
        for jt, start in enumerate(gate_windows):
            cols = slice(jt * GATE_COLS, (jt + 1) * GATE_COLS)
            xw = xc[:, start:start + GATE_ROWS].astype(BF16)
            xh = xc[:, cols]
            th_a = jnp.tanh(_mm(xw, wbd_ref[0, jt]) + 0.5 * ba_ref[:, cols])
            th_i = jnp.tanh(_mm(xw, wbd_ref[1, jt]) + 0.5 * bi_ref[:, cols])
            a = jnp.exp2(c2[:, cols] * th_a + c2[:, cols])
            hx = 0.5 * xh
            y1 = 1.0 - a * a
            u = (y1 * lax.rsqrt(jnp.maximum(y1, RSQRT_FLOOR))) * (hx * th_i + hx)
            for s in range(GATE_COLS // LANES):
                a_ref[jt * (GATE_COLS // LANES) + s] = a[:, lanes(s)]
                u_ref[jt * (GATE_COLS // LANES) + s] = u[:, lanes(s)]
            yield

        h = h_ref[...]
        for t in range(nt):
            rows = slice(t * nb, (t + 1) * nb)
            h = a_ref[:, rows, :] * h + u_ref[:, rows, :]
            u_ref[:, rows, :] = h
        h_ref[...] = h
        hlast_ref[...] = h
        yield

        if batch_major:
            hs = jnp.concatenate(
                [jnp.concatenate([u_ref[s, pl.ds(b, nt, stride=nb), :] for b in range(nb)], axis=0)
                 for s in range(n_slab)], axis=1)
        else:
            hs = jnp.concatenate([u_ref[s] for s in range(n_slab)], axis=1)
        y = (gate_ref[slot][...] * hs).astype(BF16)
        x = x_ref[...].reshape(tm, x_ref.shape[-1])
        out = _layer_norm(ALPHA * x + _mm(y, wout_ref[...]), g_ref[...], b_ref[...])
        o_ref[...] = out.reshape(o_ref.shape)
        yield

    if pipelined:
        for slot in range(2):
            @pl.when(step % 2 == slot)
            def _(slot=slot):
                _interleave(recurrent(slot), in_proj(xn_ref, 1 - slot), RGLRU_SIDE_STEPS)
    else:
        _interleave(recurrent(0))


def _interleave(main, side=(), side_per_phase=2):
    side = iter(side)
    for phase, _ in enumerate(main):
        count = side_per_phase if isinstance(side_per_phase, int) else side_per_phase[phase]
        for _ in range(count):
            next(side, None)
    for _ in side:
        pass


def _rglru(x, w_proj, w, j, conv0, h0, ln, ln_idx, *, nb):
    (w_in, j_in), (w_out, j_out) = w_proj
    cw, cb, wbd, ba, bi, lam, gate_windows = w
    ln_g, ln_b = ln
    dr, d = w_out.shape[1:]
    n_slab = dr // LANES
    hist = (CONV_W - 1) * nb
    if x.ndim == 3:
        nt = ROW_TILE // nb
        tm = ROW_TILE
        steps = x.shape[1] // nt
        block = (nb, nt, d)
        at = lambda i: (0, i, 0)
    else:
        tm = min(ROW_TILE, x.shape[0])
        steps = x.shape[0] // tm
        block = (tm, d)
        at = lambda i: (i, 0)
    assert tm % nb == 0 and tm >= hist and nb % SUBLANES == 0
    pipelined = steps > 1
    x_spec = pl.BlockSpec(block, at)
    x_specs, xs = [x_spec], [x]
    if pipelined:
        x_specs.append(pl.BlockSpec(block, lambda i: at(jnp.minimum(i + 1, steps - 1))))
        xs.append(x)
    vmem = ((2 + 2 * len(xs)) * _nbytes((tm, d), F32) + _nbytes(w_in.shape[1:], BF16)
            + _nbytes(wbd.shape[1:], BF16) + _nbytes(w_out.shape[1:], BF16)
            + 4 * _nbytes((hist + nb, dr), F32) + _nbytes((tm + hist, dr), F32)
            + 6 * _nbytes((tm, dr), F32) + _nbytes((nb, dr), F32) + 8 * _nbytes((tm, dr), F32))
    sl = _resident_slice
    return pl.pallas_call(
        functools.partial(_rglru_kernel, nb=nb, pipelined=pipelined, gate_windows=gate_windows),
        grid=(steps,),
        in_specs=x_specs + [sl(w_in.shape, j_in), sl(cw.shape, j), sl(cb.shape, j), sl(wbd.shape, j),
                            sl(ba.shape, j), sl(bi.shape, j), sl(lam.shape, j),
                            _resident(conv0.shape), _resident(h0.shape),
                            sl(w_out.shape, j_out), sl(ln_g.shape, ln_idx), sl(ln_b.shape, ln_idx)],
        out_specs=[x_spec,
                   pl.BlockSpec((n_slab, hist, LANES), lambda i: (0, 0, 0)),
                   pl.BlockSpec((n_slab, nb, LANES), lambda i: (0, 0, 0))],
        out_shape=[jax.ShapeDtypeStruct(x.shape, F32),
                   jax.ShapeDtypeStruct((n_slab, hist, LANES), F32),
                   jax.ShapeDtypeStruct((n_slab, nb, LANES), F32)],
        scratch_shapes=[pltpu.VMEM((tm, dr), F32), pltpu.VMEM((tm, dr), F32),
                        pltpu.VMEM((tm, dr), F32), pltpu.VMEM((tm, dr), F32),
                        pltpu.VMEM((n_slab, tm + hist, LANES), F32),
                        pltpu.VMEM((n_slab, tm, LANES), F32),
                        pltpu.VMEM((n_slab, tm, LANES), F32),
                        pltpu.VMEM((n_slab, nb, LANES), F32)],
        compiler_params=_params(("arbitrary",), vmem),
        name="rglru",
    )(*xs, w_in, cw, cb, wbd, ba, bi, lam, conv0, h0, w_out, ln_g, ln_b)


def _block_diag(blocks):
    n, bs, _ = blocks.shape
    cols = [jnp.pad(blocks[i], ((i * bs, (n - 1 - i) * bs), (0, 0))) for i in range(n)]
    return jnp.concatenate(cols, axis=1)


def _gate_windows(dr, block):
    starts = []
    for c0 in range(0, dr, GATE_COLS):
        lo = c0 // block * block
        hi = ((c0 + GATE_COLS - 1) // block + 1) * block
        start = min(lo // LANES * LANES, dr - GATE_ROWS)
        assert start <= lo and hi <= start + GATE_ROWS
        starts.append(start)
    return tuple(starts)


def _prep_rec(conv_w, conv_b, w_a, b_a, w_i, b_i, lam):
    n_layers, dr = conv_b.shape
    n_slab = dr // LANES
    windows = _gate_windows(dr, w_a.shape[2])
    wbd = jnp.stack([
        jnp.stack([
            jnp.stack([_block_diag(w[j])[start:start + GATE_ROWS, jt * GATE_COLS:(jt + 1) * GATE_COLS]
                       for jt, start in enumerate(windows)])
            for w in (w_a, w_i)])
        for j in range(n_layers)])
    wbd = (0.5 * wbd).astype(BF16)
    row = lambda v: v.reshape(n_layers, 1, dr)
    return (conv_w.reshape(n_layers, CONV_W, n_slab, 1, LANES),
            conv_b.reshape(n_layers, n_slab, 1, LANES), wbd, row(b_a), row(b_i), row(lam), windows)


def _to_slabs(a):
    rows, dr = a.shape
    return a.reshape(rows, dr // LANES, LANES).transpose(1, 0, 2)


def _from_slabs(a):
    n_slab, rows, _ = a.shape
    return a.transpose(1, 0, 2).reshape(rows, n_slab * LANES)


def _rope_tables(pos, dk):
    half = dk // 2
    inv = ROPE_BASE ** (-jnp.arange(half, dtype=F32) / half)
    ang = pos.astype(F32)[:, None] * inv[None, :]
    return jnp.cos(ang), jnp.sin(ang)


def _to_time_major(a, b, t):
    return a.reshape(b, t, a.shape[-1]).transpose(1, 0, 2).reshape(t * b, a.shape[-1])


def _to_batch_major(a, b, t):
    return a.reshape(t, b, a.shape[-1]).transpose(1, 0, 2).reshape(b * t, a.shape[-1])


def _trunk(x, pos, state_ret, state_conv, state_lru, ln, bank, gn_g, rec):
    b, t, d = x.shape
    m = b * t
    x = x.reshape(m, d)
    long_seq = t >= ROW_TILE
    new_ret, new_conv, new_lru = [], [], []
    for layer in range(DEPTH):
        j = layer // 2
        x = yield (x, "ffn1", layer, 3 * layer)
        if layer % 2 == 0:
            w_out, j_out = bank.get(f"ret_out_{j}")
            vd, qk = w_out.shape[1:]
            cos, sin = _rope_tables(pos, qk // RET_HEADS)
            chunk = math.gcd(t, RET_CHUNK)
            pending = bank.pending(_cast_plan()["ret_proj", j])
            (q, k, v, g), converted = _ret_proj(x, bank.get(f"ret_in_{j}"), cos, sin, qk, vd,
                                                casts=[src for _, src in pending])
            bank.put([name for name, _ in pending], converted)
            qkvg = (q.reshape(b, t, qk), k.reshape(b, t, qk), v.reshape(b, t, vd), g.reshape(b, t, vd))
            if state_ret is None:
                assert long_seq
                x, s_new = _ret_fused(*qkvg, x.reshape(b, t, d), gn_g, w_out, j_out, j, ln,
                                      3 * layer + 1, tt=ROW_TILE,
                                      chunk=math.gcd(ROW_TILE, RET_LONG_CHUNK))
                x = x.reshape(m, d)
            else:
                assert not long_seq
                gated, s_new = _ret_core(*qkvg, gn_g, j, state_ret,
                                         bb=SHORT_SEQ_BATCH_BLOCK, tt=t, chunk=chunk)
                x = _proj_ln(gated.reshape(m, vd), x, w_out, j_out, ln, 3 * layer + 1)
            new_ret.append(s_new)
        else:
            w_proj = (bank.get(f"rec_in_{j}"), bank.get(f"rec_out_{j}"))
            dr = w_proj[1][0].shape[1]
            if state_conv is None:
                conv0 = jnp.zeros((dr // LANES, (CONV_W - 1) * b, LANES), F32)
                h0 = jnp.zeros((dr // LANES, b, LANES), F32)
            else:
                conv0 = _to_slabs(state_conv[j].transpose(1, 0, 2).reshape((CONV_W - 1) * b, dr))
                h0 = _to_slabs(state_lru[j])
            x_in = x.reshape(b, t, d) if long_seq else _to_time_major(x, b, t)
            x_out, conv_n, h_last = _rglru(x_in, w_proj, rec, j, conv0, h0, ln, 3 * layer + 1, nb=b)
            x = x_out.reshape(m, d) if long_seq else _to_batch_major(x_out, b, t)
            new_conv.append(_from_slabs(conv_n).reshape(CONV_W - 1, b, dr).transpose(1, 0, 2))
            new_lru.append(_from_slabs(h_last))
        x = yield (x, "ffn2", layer, 3 * layer + 2)
    return x.reshape(b, t, d), jnp.stack(new_ret), jnp.stack(new_conv), jnp.stack(new_lru)


class _WeightBank:
    def __init__(self, f32):
        self.f32 = f32
        self.bf16 = {}

    def get(self, name):
        return self.bf16[name]

    def pending(self, names):
        return [(name, self.f32[name]) for name in names if name not in self.bf16]

    def put(self, names, arrays):
        for name, array in zip(names, arrays):
            self.bf16[name] = (array, 0)


def _cast_plan():
    plan = {}
    for layer in range(DEPTH):
        j = layer // 2
        ffn2 = [f"ffn2_in_{layer}", f"ffn2_out_{layer}"]
        if layer % 2 == 0:
            plan["ffn1", layer] = [f"ret_in_{j}", f"ret_out_{j}"]
            plan["ret_proj", j] = ffn2
        else:
            plan["ffn1", layer] = [f"rec_in_{j}", f"rec_out_{j}"] + ffn2
        plan["ffn2", layer] = ([f"ffn1_in_{layer + 1}", f"ffn1_out_{layer + 1}"]
                               if layer + 1 < DEPTH else [])
    return plan


def _run_trunks(trunks, ln, bank):
    requests = [next(g) for g in trunks]
    results = [None] * len(trunks)
    while results[0] is None:
        _, which, layer, ln_idx = requests[0]
        pending = bank.pending(_cast_plan()[which, layer])
        outs, converted = _ffn_ln([r[0] for r in requests],
                                  (bank.get(f"{which}_in_{layer}"), bank.get(f"{which}_out_{layer}")),
                                  ln, ln_idx, casts=[src for _, src in pending])
        bank.put([name for name, _ in pending], converted)
        for i, g in enumerate(trunks):
            try:
                requests[i] = g.send(outs[i])
            except StopIteration as done:
                results[i] = done.value
    return results


def kernel(x_prompt, x_sample, state_ret, state_conv, state_lru, ln_g, ln_b, ffn1_w_in, ffn1_w_out,
           ffn2_w_in, ffn2_w_out, ret_w_in, ret_gn_g, ret_w_out, rec_w_in, rec_conv_w, rec_conv_b,
           rec_w_a, rec_b_a, rec_w_i, rec_b_i, rec_lam, rec_w_out):
    d = ln_g.shape[-1]
    ln = (ln_g.reshape(-1, 1, d), ln_b.reshape(-1, 1, d))
    stacked = {"ffn1_in": ffn1_w_in, "ffn1_out": ffn1_w_out, "ffn2_in": ffn2_w_in,
               "ffn2_out": ffn2_w_out, "ret_in": ret_w_in, "ret_out": ret_w_out,
               "rec_in": rec_w_in, "rec_out": rec_w_out}
    bank = _WeightBank({f"{name}_{i}": (w, i) for name, w in stacked.items()
                        for i in range(w.shape[0])})
    first = ["ffn1_in_0", "ffn1_out_0"]
    bank.put(first, _convert([bank.f32[name] for name in first]))
    gn_g = ret_gn_g.reshape(ret_gn_g.shape[0], 1, -1)
    rec = _prep_rec(rec_conv_w, rec_conv_b, rec_w_a, rec_b_a, rec_w_i, rec_b_i, rec_lam)
    tp, ts = x_prompt.shape[1], x_sample.shape[1]
    prompt = _trunk(x_prompt, jnp.arange(tp, dtype=jnp.int32), None, None, None, ln, bank, gn_g, rec)
    sample = _trunk(x_sample, PAST_LEN + jnp.arange(ts, dtype=jnp.int32),
                    state_ret, state_conv, state_lru, ln, bank, gn_g, rec)
    (y_p, ret_p, conv_p, lru_p), (y_s, ret_s, conv_s, lru_s) = _run_trunks([prompt, sample], ln, bank)
    return (y_p, y_s, ret_p, conv_p, lru_p, ret_s, conv_s, lru_s)
```

```python
import functools
import math

import numpy as np
import jax
import jax.numpy as jnp
from jax import lax
from jax.experimental import pallas as pl
from jax.experimental.pallas import tpu as pltpu

F32 = jnp.float32
BF16 = jnp.bfloat16

DEPTH = 2
PAST_LEN = 16384
RET_HEADS = 4
RET_CHUNK = 128
RET_LONG_CHUNK = 256
ROPE_BASE = 10000.0
CONV_W = 4
LRU_C = 8.0
ALPHA = (2.0 * DEPTH) ** 0.25
LN_EPS = 1e-5
GN_EPS = 1e-6
RSQRT_FLOOR = 1e-30

V7X_SCOPED_VMEM_BYTES = 60000 * 1024
LANES = 128
SUBLANES = 8
ROW_TILE = 512
BF16_SUBLANES = 16
CAST_BLOCKS = 32
FF_CHUNK = 256
PROJ_CHUNK = 256
RGLRU_SIDE_STEPS = (1, 1, 1, 1, 1, 1, 2, 2)
GATE_COLS = 256
GATE_ROWS = 512
SHORT_SEQ_BATCH_BLOCK = 4
STATE_RING = 3


def _params(semantics, vmem_bytes):
    return pltpu.CompilerParams(
        dimension_semantics=semantics,
        vmem_limit_bytes=int(min(V7X_SCOPED_VMEM_BYTES, vmem_bytes)),
    )


def _resident(shape):
    zeros = (0,) * len(shape)
    return pl.BlockSpec(shape, lambda *_: zeros, pipeline_mode=pl.Buffered(1))


def _resident_slice(shape, idx):
    index = (idx,) + (0,) * (len(shape) - 1)
    return pl.BlockSpec((None,) + tuple(shape[1:]), lambda *_: index, pipeline_mode=pl.Buffered(1))


def _nbytes(shape, dtype):
    return int(np.prod(shape)) * jnp.dtype(dtype).itemsize


def _layer_norm(y, g, b):
    mu = jnp.mean(y, axis=-1, keepdims=True)
    d = y - mu
    var = jnp.mean(d * d, axis=-1, keepdims=True)
    return d * lax.rsqrt(var + LN_EPS) * g + b


def _mm(a, b):
    return jnp.dot(a, b, preferred_element_type=F32)


def _cast_job(src, idx, steps):
    _, r, c = src.shape
    nb = CAST_BLOCKS
    while nb > steps or r % (nb * BF16_SUBLANES) != 0:
        nb //= 2
    blk = (None, r // nb, c)
    return (pl.BlockSpec(blk, lambda s: (idx, jnp.minimum(s, nb - 1), 0)),
            pl.BlockSpec(blk, lambda s: (0, jnp.minimum(s, nb - 1), 0)),
            jax.ShapeDtypeStruct((1, r, c), BF16),
            2 * (_nbytes(blk[1:], F32) + _nbytes(blk[1:], BF16)))


def _run_casts(src_refs, dst_refs):
    for src_ref, dst_ref in zip(src_refs, dst_refs):
        dst_ref[...] = src_ref[...].astype(BF16)


def _convert_kernel(*refs):
    _run_casts(refs[:len(refs) // 2], refs[len(refs) // 2:])


def _convert(casts):
    jobs = [_cast_job(src, idx, CAST_BLOCKS) for src, idx in casts]
    return pl.pallas_call(
        _convert_kernel,
        grid=(CAST_BLOCKS,),
        in_specs=[job[0] for job in jobs],
        out_specs=[job[1] for job in jobs],
        out_shape=[job[2] for job in jobs],
        compiler_params=_params(("arbitrary",), 2 * sum(job[3] for job in jobs)),
        name="convert",
    )(*[src for src, _ in casts])


def _ffn_ln_kernel(*refs, ff, first_tile, n_cast):
    n = len(first_tile) - 1
    x_refs, (win_ref, wout_ref, g_ref, b_ref) = refs[:n], refs[n:n + 4]
    o_refs = refs[n + 4 + n_cast:2 * n + 4 + n_cast]
    _run_casts(refs[n + 4:n + 4 + n_cast], refs[2 * n + 4 + n_cast:])

    def body(x_ref, o_ref):
        x = x_ref[...]
        xb = x.astype(BF16)
        acc = jnp.zeros(x.shape, F32)
        for lo in range(0, ff, FF_CHUNK):
            gate = _mm(xb, win_ref[:, lo:lo + FF_CHUNK])
            up = _mm(xb, win_ref[:, ff + lo:ff + lo + FF_CHUNK])
            h = (gate * jax.nn.sigmoid(gate) * up).astype(BF16)
            acc = acc + _mm(h, wout_ref[lo:lo + FF_CHUNK, :])
        o_ref[...] = _layer_norm(ALPHA * x + 0.5 * acc, g_ref[...], b_ref[...])

    if n == 1:
        body(x_refs[0], o_refs[0])
    else:
        step = pl.program_id(0)
        for i in range(n):
            pl.when((step >= first_tile[i]) & (step < first_tile[i + 1]))(
                functools.partial(body, x_refs[i], o_refs[i]))


def _ffn_ln(xs, w, ln, ln_idx, casts=()):
    (w_in, i_in), (w_out, i_out) = w
    ln_g, ln_b = ln
    d = xs[0].shape[1]
    ff = w_out.shape[1]
    assert ff % FF_CHUNK == 0
    tms = [min(ROW_TILE, x.shape[0]) for x in xs]
    first_tile = [0]
    for x, tm in zip(xs, tms):
        first_tile.append(first_tile[-1] + x.shape[0] // tm)

    def rows(i):
        lo, n_tiles = first_tile[i], first_tile[i + 1] - first_tile[i]
        return pl.BlockSpec((tms[i], d), lambda s: (jnp.clip(s - lo, 0, n_tiles - 1), 0))

    specs = [rows(i) for i in range(len(xs))]
    jobs = [_cast_job(src, idx, first_tile[-1]) for src, idx in casts]
    vmem = (sum(4 * _nbytes((tm, d), F32) for tm in tms) + _nbytes(w_in.shape[1:], BF16)
            + _nbytes(w_out.shape[1:], BF16) + 8 * _nbytes((max(tms), d), F32)
            + sum(job[3] for job in jobs))
    outs = pl.pallas_call(
        functools.partial(_ffn_ln_kernel, ff=ff, first_tile=tuple(first_tile), n_cast=len(jobs)),
        grid=(first_tile[-1],),
        in_specs=specs + [_resident_slice(w_in.shape, i_in), _resident_slice(w_out.shape, i_out),
                          _resident_slice(ln_g.shape, ln_idx), _resident_slice(ln_b.shape, ln_idx)]
        + [job[0] for job in jobs],
        out_specs=specs + [job[1] for job in jobs],
        out_shape=[jax.ShapeDtypeStruct(x.shape, F32) for x in xs] + [job[2] for job in jobs],
        compiler_params=_params(("arbitrary",), vmem),
        name="ffn_ln",
    )(*xs, w_in, w_out, ln_g, ln_b, *[src for src, _ in casts])
    return outs[:len(xs)], outs[len(xs):]


def _proj_ln_kernel(a_ref, x_ref, w_ref, g_ref, b_ref, o_ref):
    y = ALPHA * x_ref[...] + _mm(a_ref[...], w_ref[...])
    o_ref[...] = _layer_norm(y, g_ref[...], b_ref[...])


def _proj_ln(a, x, w, j, ln, ln_idx):
    ln_g, ln_b = ln
    m, d = x.shape
    k = a.shape[1]
    tm = min(ROW_TILE, m)
    vmem = (2 * _nbytes((tm, k), BF16) + 4 * _nbytes((tm, d), F32) + _nbytes(w.shape[1:], BF16)
            + 4 * _nbytes((tm, d), F32))
    return pl.pallas_call(
        _proj_ln_kernel,
        grid=(m // tm,),
        in_specs=[pl.BlockSpec((tm, k), lambda i: (i, 0)), pl.BlockSpec((tm, d), lambda i: (i, 0)),
                  _resident_slice(w.shape, j),
                  _resident_slice(ln_g.shape, ln_idx), _resident_slice(ln_b.shape, ln_idx)],
        out_specs=pl.BlockSpec((tm, d), lambda i: (i, 0)),
        out_shape=jax.ShapeDtypeStruct((m, d), F32),
        compiler_params=_params(("parallel",), vmem),
        name="proj_ln",
    )(a, x, w, ln_g, ln_b)


def _ret_proj_kernel(x_ref, w_ref, cos_ref, sin_ref, *refs, heads, k_scale, n_cast):
    q_ref, k_ref, v_ref, g_ref = refs[n_cast:n_cast + 4]
    _run_casts(refs[:n_cast], refs[n_cast + 4:])
    xb = x_ref[...].astype(BF16)
    cos = cos_ref[...]
    sin = sin_ref[...]
    half = cos.shape[1]
    qk = q_ref.shape[1]
    vd = v_ref.shape[1]
    for col, dst, scale in ((0, q_ref, None), (qk, k_ref, k_scale)):
        p = _mm(xb, w_ref[:, col:col + qk])
        for h in range(heads):
            lo = 2 * half * h
            x1 = p[:, lo:lo + half]
            x2 = p[:, lo + half:lo + 2 * half]
            r1 = x1 * cos - x2 * sin
            r2 = x1 * sin + x2 * cos
            if scale is not None:
                r1 = r1 * scale
                r2 = r2 * scale
            dst[:, lo:lo + half] = r1.astype(BF16)
            dst[:, lo + half:lo + 2 * half] = r2.astype(BF16)
    v_ref[...] = _mm(xb, w_ref[:, 2 * qk:2 * qk + vd]).astype(BF16)
    g_ref[...] = _mm(xb, w_ref[:, 2 * qk + vd:]).astype(BF16)


def _ret_proj(x, w, cos, sin, qk, vd, casts=()):
    w, j = w
    m, d = x.shape
    tm = min(ROW_TILE, m)
    if cos.shape[0] < tm:
        reps = tm // cos.shape[0]
        cos, sin = jnp.tile(cos, (reps, 1)), jnp.tile(sin, (reps, 1))
    n_tab = cos.shape[0] // tm
    row = lambda n: pl.BlockSpec((tm, n), lambda i: (i, 0))
    tab = pl.BlockSpec((tm, cos.shape[1]), lambda i: (i % n_tab, 0))
    jobs = [_cast_job(src, idx, m // tm) for src, idx in casts]
    vmem = (2 * _nbytes((tm, d), F32) + _nbytes(w.shape[1:], BF16)
            + 4 * (_nbytes((tm, qk), BF16) + _nbytes((tm, vd), BF16))
            + 4 * _nbytes((tm, vd), F32) + sum(job[3] for job in jobs))
    dk = qk // RET_HEADS
    outs = pl.pallas_call(
        functools.partial(_ret_proj_kernel, heads=RET_HEADS, k_scale=dk ** -0.5, n_cast=len(jobs)),
        grid=(m // tm,),
        in_specs=[row(d), _resident_slice(w.shape, j), tab, tab] + [job[0] for job in jobs],
        out_specs=[row(qk), row(qk), row(vd), row(vd)] + [job[1] for job in jobs],
        out_shape=[jax.ShapeDtypeStruct((m, qk), BF16), jax.ShapeDtypeStruct((m, qk), BF16),
                   jax.ShapeDtypeStruct((m, vd), BF16), jax.ShapeDtypeStruct((m, vd), BF16)]
        + [job[2] for job in jobs],
        compiler_params=_params(("arbitrary",), vmem),
        name="ret_proj",
    )(x, w, cos, sin, *[src for src, _ in casts])
    return outs[:4], outs[4:]


def _retention_chunk(qc, kc, vc, s, dmask, q_dec, k_dec, chunk_dec):
    scores = lax.dot_general(qc, kc, (((1,), (1,)), ((), ())), preferred_element_type=F32) * dmask
    o = _mm(scores.astype(BF16), vc) + _mm((qc * q_dec).astype(BF16), s.astype(BF16))
    s_new = s * chunk_dec + lax.dot_general((kc * k_dec).astype(BF16), vc, (((0,), (0,)), ((), ())),
                                            preferred_element_type=F32)
    return o, s_new


def _norm_gate(o, gate, gn):
    mu = jnp.mean(o, axis=-1, keepdims=True)
    d = o - mu
    var = jnp.mean(d * d, axis=-1, keepdims=True)
    on = d * lax.rsqrt(var + GN_EPS) * gn
    gate = gate.astype(F32)
    return (gate * jax.nn.sigmoid(gate) * on).astype(BF16)


def _ret_core_kernel(q_ref, k_ref, v_ref, g_ref, gn_ref, dm_ref, qd_ref, kd_ref, cd_ref, s0_hbm,
                     o_ref, s_ref, ring_ref, sem, *, heads, layer, n_steps):
    bb = q_ref.shape[0]
    dk = q_ref.shape[2] // heads
    dv = v_ref.shape[2] // heads
    step = pl.program_id(0)

    def fetch(at_step, slot):
        return pltpu.make_async_copy(s0_hbm.at[layer, pl.ds(at_step * bb, bb)],
                                     ring_ref.at[slot], sem.at[slot])

    @pl.when(step == 0)
    def _():
        for s in range(min(STATE_RING - 1, n_steps)):
            fetch(s, s).start()

    slot = step % STATE_RING
    fetch(step, slot).wait()
    ahead = step + STATE_RING - 1

    @pl.when(ahead < n_steps)
    def _():
        fetch(ahead, ahead % STATE_RING).start()

    for b in range(bb):
        for h in range(heads):
            qk_cols = slice(h * dk, (h + 1) * dk)
            v_cols = slice(h * dv, (h + 1) * dv)
            o, s_ref[b, h] = _retention_chunk(
                q_ref[b, :, qk_cols], k_ref[b, :, qk_cols], v_ref[b, :, v_cols],
                ring_ref[slot, b, h], dm_ref[h], qd_ref[h], kd_ref[h], cd_ref[h])
            o_ref[b, :, v_cols] = _norm_gate(o, g_ref[b, :, v_cols], gn_ref[:, v_cols])


def _ret_fused_kernel(q_ref, k_ref, v_ref, g_ref, x_ref, gn_ref, dm_ref, qd_ref, kd_ref, cd_ref,
                      wout_ref, lng_ref, lnb_ref, o_ref, s_ref, ret0_ref, ret1_ref, gated_ref,
                      *, heads, chunk, tiles_per_row, n_tiles):
    ret_ref = (ret0_ref, ret1_ref)
    tt = q_ref.shape[1]
    dk = q_ref.shape[2] // heads
    dv = v_ref.shape[2] // heads
    d_model = o_ref.shape[2]
    step = pl.program_id(0)
    chunks = [slice(c * chunk, (c + 1) * chunk) for c in range(tt // chunk)]
    v_cols = lambda h: slice(h * dv, (h + 1) * dv)

    @pl.when((step % tiles_per_row == 0) & (step < n_tiles))
    def _():
        s_ref[...] = jnp.zeros(s_ref.shape, F32)

    def attend(slot):
        for rows in chunks:
            for h in range(heads):
                qk_cols = slice(h * dk, (h + 1) * dk)
                ret_ref[slot][rows, v_cols(h)], s_ref[0, h] = _retention_chunk(
                    q_ref[0, rows, qk_cols], k_ref[0, rows, qk_cols], v_ref[0, rows, v_cols(h)],
                    s_ref[0, h], dm_ref[h], qd_ref[h], kd_ref[h], cd_ref[h])
                yield

    def finish(slot):
        for h in range(heads):
            for rows in chunks:
                gated_ref[rows, v_cols(h)] = _norm_gate(
                    ret_ref[slot][rows, v_cols(h)], g_ref[0, rows, v_cols(h)], gn_ref[:, v_cols(h)])
                yield
            for lo in range(0, d_model, PROJ_CHUNK):
                cols = slice(lo, lo + PROJ_CHUNK)
                part = _mm(gated_ref[:, v_cols(h)], wout_ref[v_cols(h), cols])
                o_ref[0, :, cols] = part if h == 0 else o_ref[0, :, cols] + part
                yield
        o_ref[0] = _layer_norm(ALPHA * x_ref[0] + o_ref[0], lng_ref[...], lnb_ref[...])
        yield

    @pl.when(step == 0)
    def _():
        _interleave(attend(0))

    for slot in range(2):
        @pl.when((step > 0) & (step < n_tiles) & (step % 2 == slot))
        def _(slot=slot):
            _interleave(attend(slot), finish(1 - slot))

    @pl.when(step == n_tiles)
    def _():
        _interleave(finish((n_tiles - 1) % 2))


def _decay_tables(chunk):
    lg = jnp.log1p(-jnp.exp2(-5.0 - jnp.arange(RET_HEADS, dtype=F32)))
    idx = jnp.arange(chunk, dtype=F32)
    rel = idx[:, None] - idx[None, :]
    dmask = jnp.where(rel >= 0, jnp.exp(lg[:, None, None] * jnp.maximum(rel, 0.0)), 0.0)
    q_dec = jnp.exp(lg[:, None] * (idx + 1.0))[:, :, None]
    k_dec = jnp.exp(lg[:, None] * (chunk - 1.0 - idx))[:, :, None]
    chunk_dec = jnp.exp(lg * chunk)[:, None, None]
    return dmask, q_dec, k_dec, chunk_dec


def _ret_core(q, k, v, g, gn_g, j, s0, *, bb, tt, chunk):
    b, t, qk = q.shape
    vd = v.shape[2]
    heads = RET_HEADS
    dk, dv = qk // heads, vd // heads
    assert tt == t == chunk
    tables = _decay_tables(chunk)
    blk = lambda n: pl.BlockSpec((bb, tt, n), lambda i: (i, 0, 0))
    state = (bb, heads, dk, dv)
    vmem = (4 * (_nbytes((bb, tt, qk), BF16) + _nbytes((bb, tt, vd), BF16))
            + 2 * _nbytes((bb, tt, vd), BF16) + (2 + STATE_RING) * _nbytes(state, F32)
            + 16 * _nbytes((max(chunk, SUBLANES), dv), F32) + 8 * _nbytes((dk, dv), F32))
    return pl.pallas_call(
        functools.partial(_ret_core_kernel, heads=heads, layer=j, n_steps=b // bb),
        grid=(b // bb,),
        in_specs=[blk(qk), blk(qk), blk(vd), blk(vd), _resident_slice(gn_g.shape, j)]
        + [_resident(tab.shape) for tab in tables]
        + [pl.BlockSpec(memory_space=pl.ANY)],
        out_specs=[blk(vd), pl.BlockSpec(state, lambda i: (i, 0, 0, 0))],
        out_shape=[jax.ShapeDtypeStruct((b, t, vd), BF16),
                   jax.ShapeDtypeStruct((b, heads, dk, dv), F32)],
        scratch_shapes=[pltpu.VMEM((STATE_RING,) + state, F32),
                        pltpu.SemaphoreType.DMA((STATE_RING,))],
        compiler_params=_params(("arbitrary",), vmem),
        name="ret_core",
    )(q, k, v, g, gn_g, *tables, s0)


def _ret_fused(q, k, v, g, x, gn_g, w_out, j_out, j, ln, ln_idx, *, tt, chunk):
    ln_g, ln_b = ln
    b, t, qk = q.shape
    vd, d = v.shape[2], x.shape[2]
    heads = RET_HEADS
    dk, dv = qk // heads, vd // heads
    tables = _decay_tables(chunk)
    tiles_per_row = t // tt
    n_tiles = b * tiles_per_row
    assert tiles_per_row >= 2

    def tile_block(n, lag):
        def index(i):
            tile = jnp.clip(i - lag, 0, n_tiles - 1)
            return (tile // tiles_per_row, tile % tiles_per_row, 0)
        return pl.BlockSpec((1, tt, n), index)

    state = pl.BlockSpec((1, heads, dk, dv),
                         lambda i: (jnp.minimum(i, n_tiles - 1) // tiles_per_row, 0, 0, 0))
    vmem = (4 * (_nbytes((tt, qk), BF16) + _nbytes((tt, vd), BF16)) + 2 * _nbytes((tt, vd), BF16)
            + 4 * _nbytes((tt, d), F32) + 2 * _nbytes((heads, dk, dv), F32)
            + _nbytes(w_out.shape[1:], BF16) + 2 * _nbytes((tt, vd), F32) + _nbytes((tt, vd), BF16)
            + 16 * _nbytes((chunk, dv), F32) + 8 * _nbytes((dk, dv), F32) + 4 * _nbytes((tt, d), F32))
    return pl.pallas_call(
        functools.partial(_ret_fused_kernel, heads=heads, chunk=chunk, tiles_per_row=tiles_per_row,
                          n_tiles=n_tiles),
        grid=(n_tiles + 1,),
        in_specs=[tile_block(qk, 0), tile_block(qk, 0), tile_block(vd, 0), tile_block(vd, 1),
                  tile_block(d, 1), _resident_slice(gn_g.shape, j)]
        + [_resident(tab.shape) for tab in tables]
        + [_resident_slice(w_out.shape, j_out), _resident_slice(ln_g.shape, ln_idx),
           _resident_slice(ln_b.shape, ln_idx)],
        out_specs=[tile_block(d, 1), state],
        out_shape=[jax.ShapeDtypeStruct((b, t, d), F32),
                   jax.ShapeDtypeStruct((b, heads, dk, dv), F32)],
        scratch_shapes=[pltpu.VMEM((tt, vd), F32), pltpu.VMEM((tt, vd), F32),
                        pltpu.VMEM((tt, vd), BF16)],
        compiler_params=_params(("arbitrary",), vmem),
        name="ret_fused",
    )(q, k, v, g, x, gn_g, *tables, w_out, ln_g, ln_b)


def _gelu_tanh(x):
    c = math.sqrt(2.0 / math.pi)
    th = jnp.tanh(x * ((x * x) * (c * 0.044715) + c))
    hx = 0.5 * x
    return hx * th + hx


def _rglru_kernel(*refs, nb, pipelined, gate_windows):
    if pipelined:
        x_ref, xn_ref = refs[:2]
        refs = refs[2:]
    else:
        x_ref, xn_ref = refs[0], None
        refs = refs[1:]
    (win_ref, cw_ref, cb_ref, wbd_ref, ba_ref, bi_ref, lam_ref, conv0_ref, h0_ref, wout_ref,
     g_ref, b_ref, o_ref, convn_ref, hlast_ref,
     gate0_ref, gate1_ref, cin0_ref, cin1_ref, xpad_ref, a_ref, u_ref, h_ref) = refs
    gate_ref = (gate0_ref, gate1_ref)
    cin_ref = (cin0_ref, cin1_ref)
    batch_major = len(x_ref.shape) == 3
    dr = wout_ref.shape[0]
    n_slab = dr // LANES
    hist = (CONV_W - 1) * nb
    tm = a_ref.shape[1]
    nt = tm // nb
    lanes = lambda s: slice(s * LANES, (s + 1) * LANES)
    step = pl.program_id(0)

    def in_proj(src_ref, slot):
        xb = src_ref[...].reshape(tm, src_ref.shape[-1]).astype(BF16)
        for lo in range(0, dr, PROJ_CHUNK):
            cin_ref[slot][:, lo:lo + PROJ_CHUNK] = _mm(xb, win_ref[:, dr + lo:dr + lo + PROJ_CHUNK])
            yield
        for lo in range(0, dr, PROJ_CHUNK):
            gate_ref[slot][:, lo:lo + PROJ_CHUNK] = _gelu_tanh(_mm(xb, win_ref[:, lo:lo + PROJ_CHUNK]))
            yield

    @pl.when(step == 0)
    def _():
        xpad_ref[:, 0:hist, :] = conv0_ref[...]
        h_ref[...] = h0_ref[...]
        _interleave(in_proj(x_ref, 0))

    def recurrent(slot):
        for s in range(n_slab):
            if batch_major:
                for b in range(nb):
                    xpad_ref[s, pl.ds(hist + b, nt, stride=nb), :] = (
                        cin_ref[slot][b * nt:(b + 1) * nt, lanes(s)])
            else:
                xpad_ref[s, hist:hist + tm, :] = cin_ref[slot][:, lanes(s)]

        conv = None
        for j in range(CONV_W):
            term = cw_ref[j] * xpad_ref[:, j * nb:j * nb + tm, :]
            conv = term if conv is None else conv + term
        conv = cb_ref[...] + conv
        xc = jnp.concatenate([conv[s] for s in range(n_slab)], axis=1)

        tail = xpad_ref[:, tm:tm + hist, :]
        convn_ref[...] = tail
        xpad_ref[:, 0:hist, :] = tail
        yield

        lam = lam_ref[...]
        softplus_neg_lam = jnp.maximum(-lam, 0.0) + jnp.log1p(jnp.exp(-jnp.abs(lam)))
        c2 = (-0.5 * LRU_C * math.log2(math.e)) * softplus_neg_lam
        for jt, start in enumerate(gate_windows):
            cols = slice(jt * GATE_COLS, (jt + 1) * GATE_COLS)
            xw = xc[:, start:start + GATE_ROWS].astype(BF16)
            xh = xc[:, cols]
            th_a = jnp.tanh(_mm(xw, wbd_ref[0, jt]) + 0.5 * ba_ref[:, cols])
            th_i = jnp.tanh(_mm(xw, wbd_ref[1, jt]) + 0.5 * bi_ref[:, cols])
            a = jnp.exp2(c2[:, cols] * th_a + c2[:, cols])
            hx = 0.5 * xh
            y1 = 1.0 - a * a
            u = (y1 * lax.rsqrt(jnp.maximum(y1, RSQRT_FLOOR))) * (hx * th_i + hx)
            for s in range(GATE_COLS // LANES):
                a_ref[jt * (GATE_COLS // LANES) + s] = a[:, lanes(s)]
                u_ref[jt * (GATE_COLS // LANES) + s] = u[:, lanes(s)]
            yield

        h = h_ref[...]
        for t in range(nt):
            rows = slice(t * nb, (t + 1) * nb)
            h = a_ref[:, rows, :] * h + u_ref[:, rows, :]
            u_ref[:, rows, :] = h
        h_ref[...] = h
        hlast_ref[...] = h
        yield

        if batch_major:
            hs = jnp.concatenate(
                [jnp.concatenate([u_ref[s, pl.ds(b, nt, stride=nb), :] for b in range(nb)], axis=0)
                 for s in range(n_slab)], axis=1)
        else:
            hs = jnp.concatenate([u_ref[s] for s in range(n_slab)], axis=1)
        y = (gate_ref[slot][...] * hs).astype(BF16)
        x = x_ref[...].reshape(tm, x_ref.shape[-1])
        out = _layer_norm(ALPHA * x + _mm(y, wout_ref[...]), g_ref[...], b_ref[...])
        o_ref[...] = out.reshape(o_ref.shape)
        yield

    if pipelined:
        for slot in range(2):
            @pl.when(step % 2 == slot)
            def _(slot=slot):
                _interleave(recurrent(slot), in_proj(xn_ref, 1 - slot), RGLRU_SIDE_STEPS)
    else:
        _interleave(recurrent(0))


def _interleave(main, side=(), side_per_phase=2):
    side = iter(side)
    for phase, _ in enumerate(main):
        count = side_per_phase if isinstance(side_per_phase, int) else side_per_phase[phase]
        for _ in range(count):
            next(side, None)
    for _ in side:
        pass


def _rglru(x, w_proj, w, j, conv0, h0, ln, ln_idx, *, nb):
    (w_in, j_in), (w_out, j_out) = w_proj
    cw, cb, wbd, ba, bi, lam, gate_windows = w
    ln_g, ln_b = ln
    dr, d = w_out.shape[1:]
    n_slab = dr // LANES
    hist = (CONV_W - 1) * nb
    if x.ndim == 3:
        nt = ROW_TILE // nb
        tm = ROW_TILE
        steps = x.shape[1] // nt
        block = (nb, nt, d)
        at = lambda i: (0, i, 0)
    else:
        tm = min(ROW_TILE, x.shape[0])
        steps = x.shape[0] // tm
        block = (tm, d)
        at = lambda i: (i, 0)
    assert tm % nb == 0 and tm >= hist and nb % SUBLANES == 0
    pipelined = steps > 1
    x_spec = pl.BlockSpec(block, at)
    x_specs, xs = [x_spec], [x]
    if pipelined:
        x_specs.append(pl.BlockSpec(block, lambda i: at(jnp.minimum(i + 1, steps - 1))))
        xs.append(x)
    vmem = ((2 + 2 * len(xs)) * _nbytes((tm, d), F32) + _nbytes(w_in.shape[1:], BF16)
            + _nbytes(wbd.shape[1:], BF16) + _nbytes(w_out.shape[1:], BF16)
            + 4 * _nbytes((hist + nb, dr), F32) + _nbytes((tm + hist, dr), F32)
            + 6 * _nbytes((tm, dr), F32) + _nbytes((nb, dr), F32) + 8 * _nbytes((tm, dr), F32))
    sl = _resident_slice
    return pl.pallas_call(
        functools.partial(_rglru_kernel, nb=nb, pipelined=pipelined, gate_windows=gate_windows),
        grid=(steps,),
        in_specs=x_specs + [sl(w_in.shape, j_in), sl(cw.shape, j), sl(cb.shape, j), sl(wbd.shape, j),
                            sl(ba.shape, j), sl(bi.shape, j), sl(lam.shape, j),
                            _resident(conv0.shape), _resident(h0.shape),
                            sl(w_out.shape, j_out), sl(ln_g.shape, ln_idx), sl(ln_b.shape, ln_idx)],
        out_specs=[x_spec,
                   pl.BlockSpec((n_slab, hist, LANES), lambda i: (0, 0, 0)),
                   pl.BlockSpec((n_slab, nb, LANES), lambda i: (0, 0, 0))],
        out_shape=[jax.ShapeDtypeStruct(x.shape, F32),
                   jax.ShapeDtypeStruct((n_slab, hist, LANES), F32),
                   jax.ShapeDtypeStruct((n_slab, nb, LANES), F32)],
        scratch_shapes=[pltpu.VMEM((tm, dr), F32), pltpu.VMEM((tm, dr), F32),
                        pltpu.VMEM((tm, dr), F32), pltpu.VMEM((tm, dr), F32),
                        pltpu.VMEM((n_slab, tm + hist, LANES), F32),
                        pltpu.VMEM((n_slab, tm, LANES), F32),
                        pltpu.VMEM((n_slab, tm, LANES), F32),
                        pltpu.VMEM((n_slab, nb, LANES), F32)],
        compiler_params=_params(("arbitrary",), vmem),
        name="rglru",
    )(*xs, w_in, cw, cb, wbd, ba, bi, lam, conv0, h0, w_out, ln_g, ln_b)


def _block_diag(blocks):
    n, bs, _ = blocks.shape
    cols = [jnp.pad(blocks[i], ((i * bs, (n - 1 - i) * bs), (0, 0))) for i in range(n)]
    return jnp.concatenate(cols, axis=1)


def _gate_windows(dr, block):
    starts = []
    for c0 in range(0, dr, GATE_COLS):
        lo = c0 // block * block
        hi = ((c0 + GATE_COLS - 1) // block + 1) * block
        start = min(lo // LANES * LANES, dr - GATE_ROWS)
        assert start <= lo and hi <= start + GATE_ROWS
        starts.append(start)
    return tuple(starts)


def _prep_rec(conv_w, conv_b, w_a, b_a, w_i, b_i, lam):
    n_layers, dr = conv_b.shape
    n_slab = dr // LANES
    windows = _gate_windows(dr, w_a.shape[2])
    wbd = jnp.stack([
        jnp.stack([
            jnp.stack([_block_diag(w[j])[start:start + GATE_ROWS, jt * GATE_COLS:(jt + 1) * GATE_COLS]
                       for jt, start in enumerate(windows)])
            for w in (w_a, w_i)])
        for j in range(n_layers)])
    wbd = (0.5 * wbd).astype(BF16)
    row = lambda v: v.reshape(n_layers, 1, dr)
    return (conv_w.reshape(n_layers, CONV_W, n_slab, 1, LANES),
            conv_b.reshape(n_layers, n_slab, 1, LANES), wbd, row(b_a), row(b_i), row(lam), windows)


def _to_slabs(a):
    rows, dr = a.shape
    return a.reshape(rows, dr // LANES, LANES).transpose(1, 0, 2)


def _from_slabs(a):
    n_slab, rows, _ = a.shape
    return a.transpose(1, 0, 2).reshape(rows, n_slab * LANES)


def _rope_tables(pos, dk):
    half = dk // 2
    inv = ROPE_BASE ** (-jnp.arange(half, dtype=F32) / half)
    ang = pos.astype(F32)[:, None] * inv[None, :]
    return jnp.cos(ang), jnp.sin(ang)


def _to_time_major(a, b, t):
    return a.reshape(b, t, a.shape[-1]).transpose(1, 0, 2).reshape(t * b, a.shape[-1])


def _to_batch_major(a, b, t):
    return a.reshape(t, b, a.shape[-1]).transpose(1, 0, 2).reshape(b * t, a.shape[-1])


def _trunk(x, pos, state_ret, state_conv, state_lru, ln, bank, gn_g, rec):
    b, t, d = x.shape
    m = b * t
    x = x.reshape(m, d)
    long_seq = t >= ROW_TILE
    new_ret, new_conv, new_lru = [], [], []
    for layer in range(DEPTH):
        j = layer // 2
        x = yield (x, "ffn1", layer, 3 * layer)
        if layer % 2 == 0:
            w_out, j_out = bank.get(f"ret_out_{j}")
            vd, qk = w_out.shape[1:]
            cos, sin = _rope_tables(pos, qk // RET_HEADS)
            chunk = math.gcd(t, RET_CHUNK)
            pending = bank.pending(_cast_plan()["ret_proj", j])
            (q, k, v, g), converted = _ret_proj(x, bank.get(f"ret_in_{j}"), cos, sin, qk, vd,
                                                casts=[src for _, src in pending])
            bank.put([name for name, _ in pending], converted)
            qkvg = (q.reshape(b, t, qk), k.reshape(b, t, qk), v.reshape(b, t, vd), g.reshape(b, t, vd))
            if state_ret is None:
                assert long_seq
                x, s_new = _ret_fused(*qkvg, x.reshape(b, t, d), gn_g, w_out, j_out, j, ln,
                                      3 * layer + 1, tt=ROW_TILE,
                                      chunk=math.gcd(ROW_TILE, RET_LONG_CHUNK))
                x = x.reshape(m, d)
            else:
                assert not long_seq
                gated, s_new = _ret_core(*qkvg, gn_g, j, state_ret,
                                         bb=SHORT_SEQ_BATCH_BLOCK, tt=t, chunk=chunk)
                x = _proj_ln(gated.reshape(m, vd), x, w_out, j_out, ln, 3 * layer + 1)
            new_ret.append(s_new)
        else:
            w_proj = (bank.get(f"rec_in_{j}"), bank.get(f"rec_out_{j}"))
            dr = w_proj[1][0].shape[1]
            if state_conv is None:
                conv0 = jnp.zeros((dr // LANES, (CONV_W - 1) * b, LANES), F32)
                h0 = jnp.zeros((dr // LANES, b, LANES), F32)
            else:
                conv0 = _to_slabs(state_conv[j].transpose(1, 0, 2).reshape((CONV_W - 1) * b, dr))
                h0 = _to_slabs(state_lru[j])
            x_in = x.reshape(b, t, d) if long_seq else _to_time_major(x, b, t)
            x_out, conv_n, h_last = _rglru(x_in, w_proj, rec, j, conv0, h0, ln, 3 * layer + 1, nb=b)
            x = x_out.reshape(m, d) if long_seq else _to_batch_major(x_out, b, t)
            new_conv.append(_from_slabs(conv_n).reshape(CONV_W - 1, b, dr).transpose(1, 0, 2))
            new_lru.append(_from_slabs(h_last))
        x = yield (x, "ffn2", layer, 3 * layer + 2)
    return x.reshape(b, t, d), jnp.stack(new_ret), jnp.stack(new_conv), jnp.stack(new_lru)


class _WeightBank:
    def __init__(self, f32):
        self.f32 = f32
        self.bf16 = {}

    def get(self, name):
        return self.bf16[name]

    def pending(self, names):
        return [(name, self.f32[name]) for name in names if name not in self.bf16]

    def put(self, names, arrays):
        for name, array in zip(names, arrays):
            self.bf16[name] = (array, 0)


def _cast_plan():
    plan = {}
    for layer in range(DEPTH):
        j = layer // 2
        ffn2 = [f"ffn2_in_{layer}", f"ffn2_out_{layer}"]
        if layer % 2 == 0:
            plan["ffn1", layer] = [f"ret_in_{j}", f"ret_out_{j}"]
            plan["ret_proj", j] = ffn2
        else:
            plan["ffn1", layer] = [f"rec_in_{j}", f"rec_out_{j}"] + ffn2
        plan["ffn2", layer] = ([f"ffn1_in_{layer + 1}", f"ffn1_out_{layer + 1}"]
                               if layer + 1 < DEPTH else [])
    return plan


def _run_trunks(trunks, ln, bank):
    requests = [next(g) for g in trunks]
    results = [None] * len(trunks)
    while results[0] is None:
        _, which, layer, ln_idx = requests[0]
        pending = bank.pending(_cast_plan()[which, layer])
        outs, converted = _ffn_ln([r[0] for r in requests],
                                  (bank.get(f"{which}_in_{layer}"), bank.get(f"{which}_out_{layer}")),
                                  ln, ln_idx, casts=[src for _, src in pending])
        bank.put([name for name, _ in pending], converted)
        for i, g in enumerate(trunks):
            try:
                requests[i] = g.send(outs[i])
            except StopIteration as done:
                results[i] = done.value
    return results


def kernel(x_prompt, x_sample, state_ret, state_conv, state_lru, ln_g, ln_b, ffn1_w_in, ffn1_w_out,
           ffn2_w_in, ffn2_w_out, ret_w_in, ret_gn_g, ret_w_out, rec_w_in, rec_conv_w, rec_conv_b,
           rec_w_a, rec_b_a, rec_w_i, rec_b_i, rec_lam, rec_w_out):
    d = ln_g.shape[-1]
    ln = (ln_g.reshape(-1, 1, d), ln_b.reshape(-1, 1, d))
    stacked = {"ffn1_in": ffn1_w_in, "ffn1_out": ffn1_w_out, "ffn2_in": ffn2_w_in,
               "ffn2_out": ffn2_w_out, "ret_in": ret_w_in, "ret_out": ret_w_out,
               "rec_in": rec_w_in, "rec_out": rec_w_out}
    bank = _WeightBank({f"{name}_{i}": (w, i) for name, w in stacked.items()
                        for i in range(w.shape[0])})
    first = ["ffn1_in_0", "ffn1_out_0"]
    bank.put(first, _convert([bank.f32[name] for name in first]))
    gn_g = ret_gn_g.reshape(ret_gn_g.shape[0], 1, -1)
    rec = _prep_rec(rec_conv_w, rec_conv_b, rec_w_a, rec_b_a, rec_w_i, rec_b_i, rec_lam)
    tp, ts = x_prompt.shape[1], x_sample.shape[1]
    prompt = _trunk(x_prompt, jnp.arange(tp, dtype=jnp.int32), None, None, None, ln, bank, gn_g, rec)
    sample = _trunk(x_sample, PAST_LEN + jnp.arange(ts, dtype=jnp.int32),
                    state_ret, state_conv, state_lru, ln, bank, gn_g, rec)
    (y_p, ret_p, conv_p, lru_p), (y_s, ret_s, conv_s, lru_s) = _run_trunks([prompt, sample], ln, bank)
    return (y_p, y_s, ret_p, conv_p, lru_p, ret_s, conv_s, lru_s)
```
